```python
import jax, jax.numpy as jnp
from jax import lax
import numpy as np

D_MODEL = 1024
BATCH = 8
SEQ = 4096
DEPTH = 2
DEC_BATCH = 32
DEC_SEQ = 4
PAST_LEN = 16384
PAGE_SIZE = 128

HEAD_DIM = 64
A_HEADS = D_MODEL // 256
A_KEY_DIM = HEAD_DIM
A_VAL_DIM = HEAD_DIM
A_KEY_WIDTH = A_HEADS * A_KEY_DIM
A_WIDTH = A_HEADS * A_VAL_DIM
HGRN_CHUNK = 64
B_HEADS = D_MODEL // 128
B_WIDTH = B_HEADS * HEAD_DIM
FOX_Q_BLOCK = 128
FOX_F_BIAS_INIT = 3.0
C_GROUPS = D_MODEL // 256
C_WIDTH = C_GROUPS * HEAD_DIM
SGU_CHUNK = 128
D_MIX = A_WIDTH + B_WIDTH + C_WIDTH
IN_COLS = 2 * A_KEY_WIDTH + 2 * A_WIDTH + 3 * B_WIDTH + B_HEADS + 2 * C_WIDTH
N_EXPERT_GROUPS = 4
EXPERTS_PER_GROUP = 4
EXPERT_FF = D_MODEL // 4
TOP_K_FINE = 2
RMS_EPS = 1e-6
NEG_INF = -1e30

kernel_name = 'hymba_hgrn2_fox_sgu_hiermoe_step'


def rmsnorm(x, g):
    xf = x.astype(jnp.float32)
    y = xf * lax.rsqrt(jnp.mean(xf * xf, axis=-1, keepdims=True) + RMS_EPS)
    return (y * g.astype(jnp.float32)).astype(x.dtype)


def group_rmsnorm(x, g, n_groups):
    shp = x.shape
    xg = x.reshape(shp[:-1] + (n_groups, shp[-1] // n_groups)).astype(jnp.float32)
    y = xg * lax.rsqrt(jnp.mean(xg * xg, axis=-1, keepdims=True) + RMS_EPS)
    return (y.reshape(shp) * g.astype(jnp.float32)).astype(x.dtype)


def hgrn2_chunked(q, logf, k, i, s0):
    B_, L, H, DK = q.shape
    DV = i.shape[-1]
    C = min(HGRN_CHUNK, L)
    n = -(-L // C)
    pad = n * C - L

    def prep(a):
        a = jnp.pad(a.astype(jnp.float32), ((0, 0), (0, pad), (0, 0), (0, 0)))
        return a.reshape(B_, n, C, H, a.shape[-1]).swapaxes(0, 1)

    qc, lfc, kc, ic = prep(q), prep(logf), prep(k), prep(i)
    causal = jnp.tril(jnp.ones((C, C), dtype=bool))[None, :, :, None, None]

    def step(S, blk):
        qb, lfb, kb, ib = blk
        b = jnp.cumsum(lfb, axis=1)
        o_inter = jnp.einsum('bthk,bhkv->bthv', qb * jnp.exp(b), S)
        diff = b[:, :, None] - b[:, None, :]
        decay = jnp.where(causal, jnp.exp(jnp.where(causal, diff, 0.0)), 0.0)
        A = jnp.einsum('bthk,btshk,bshk->bhts', qb, decay, kb)
        o_intra = jnp.einsum('bhts,bshv->bthv', A, ib)
        b_last = b[:, -1]
        S_new = jnp.exp(b_last)[..., None] * S + jnp.einsum(
            'bshk,bshv->bhkv', kb * jnp.exp(b_last[:, None] - b), ib)
        return S_new, o_inter + o_intra

    S_fin, o = lax.scan(step, s0.astype(jnp.float32), (qc, lfc, kc, ic))
    o = o.swapaxes(0, 1).reshape(B_, n * C, H, DV)[:, :L]
    return o, S_fin


def fox_logits(q, k, cq, ck, qpos, kpos):
    s = jnp.einsum('bqhd,bkhd->bhqk', q, k).astype(jnp.float32) * (HEAD_DIM ** -0.5)
    s = s + jnp.swapaxes(cq, 1, 2)[..., :, None] - jnp.swapaxes(ck, 1, 2)[..., None, :]
    return jnp.where(kpos[None, :] <= qpos[:, None], s, NEG_INF)


def sgu_mix(v, w_s, b):
    B_, L, G, HD = v.shape
    n = -(-L // SGU_CHUNK)
    pad = n * SGU_CHUNK - L
    vp = jnp.pad(v, ((0, 0), (0, pad), (0, 0), (0, 0))).reshape(B_, n, SGU_CHUNK, G, HD)
    w_causal = w_s * jnp.tril(jnp.ones((SGU_CHUNK, SGU_CHUNK), w_s.dtype))
    z = jnp.einsum('gts,bnsgc->bntgc', w_causal, vp) + b.T[None, None, :, :, None]
    return z.reshape(B_, n * SGU_CHUNK, G, HD)[:, :L]


def token_mixers(h, p, lb, s0, past_k, past_v, past_logf):
    B_, L, _ = h.shape
    f32 = jnp.float32
    sizes = (A_KEY_WIDTH, A_KEY_WIDTH, A_WIDTH, A_WIDTH, B_WIDTH, B_WIDTH, B_WIDTH, B_HEADS, C_WIDTH, C_WIDTH)
    offs = [int(o) for o in np.cumsum(sizes)[:-1]]
    a_q, a_f, a_i, a_g, b_q, b_k, b_v, b_f, c_u, c_v = jnp.split(h @ p['w_in'], offs, axis=-1)

    lb_h = lb.reshape(A_HEADS, A_KEY_DIM)
    z = a_f.reshape(B_, L, A_HEADS, A_KEY_DIM).astype(f32)
    log_f_a = jnp.logaddexp(jnp.log(lb_h), jnp.log1p(-lb_h) + jax.nn.log_sigmoid(z))
    k_a = (1.0 - lb_h) * jax.nn.sigmoid(-z)
    q_a = jax.nn.silu(a_q).reshape(B_, L, A_HEADS, A_KEY_DIM)
    i_a = a_i.reshape(B_, L, A_HEADS, A_VAL_DIM)
    o_a, s_new = hgrn2_chunked(q_a, log_f_a, k_a, i_a, s0)
    o_a = group_rmsnorm(o_a.reshape(B_, L, A_WIDTH).astype(h.dtype), p['hgrn_out_gain'], A_HEADS) * jax.nn.silu(a_g)

    q_b = b_q.reshape(B_, L, B_HEADS, HEAD_DIM)
    k_b = b_k.reshape(B_, L, B_HEADS, HEAD_DIM)
    v_b = b_v.reshape(B_, L, B_HEADS, HEAD_DIM)
    log_f_b = jax.nn.log_sigmoid(b_f.astype(f32) + p['fox_f_bias'].astype(f32))
    if past_k is None:
        c = jnp.cumsum(log_f_b, axis=1)
        nb = L // FOX_Q_BLOCK
        qb = q_b.reshape(B_, nb, FOX_Q_BLOCK, B_HEADS, HEAD_DIM).swapaxes(0, 1)
        cb = c.reshape(B_, nb, FOX_Q_BLOCK, B_HEADS).swapaxes(0, 1)
        kpos = jnp.arange(L)

        def block(args):
            qi, ci, bi = args
            pr = jax.nn.softmax(fox_logits(qi, k_b, ci, c, bi * FOX_Q_BLOCK + jnp.arange(FOX_Q_BLOCK), kpos), axis=-1)
            return jnp.einsum('bhqk,bkhd->bqhd', pr.astype(v_b.dtype), v_b)

        o_b = lax.map(block, (qb, cb, jnp.arange(nb))).swapaxes(0, 1).reshape(B_, L, B_HEADS, HEAD_DIM)
    else:
        P = past_k.shape[1]
        c_past = jnp.cumsum(past_logf.astype(f32), axis=1)
        c_new = c_past[:, -1:] + jnp.cumsum(log_f_b, axis=1)
        qpos = P + jnp.arange(L)
        s_all = jnp.concatenate([
            fox_logits(q_b, past_k, c_new, c_past, qpos, jnp.arange(P)),
            fox_logits(q_b, k_b, c_new, c_new, qpos, qpos)], axis=-1)
        pr = jax.nn.softmax(s_all, axis=-1).astype(v_b.dtype)
        o_b = (jnp.einsum('bhqk,bkhd->bqhd', pr[..., :P], past_v)
               + jnp.einsum('bhqk,bkhd->bqhd', pr[..., P:], v_b))
    o_b = group_rmsnorm(o_b.reshape(B_, L, B_WIDTH), p['fox_out_gain'], B_HEADS)

    u = jax.nn.gelu(c_u)
    vn = group_rmsnorm(jax.nn.gelu(c_v), p['sgu_v_gain'], C_GROUPS)
    zc = sgu_mix(vn.reshape(B_, L, C_GROUPS, HEAD_DIM), p['sgu_w_s'], p['sgu_b'])
    o_c = group_rmsnorm(u * zc.reshape(B_, L, C_WIDTH), p['sgu_out_gain'], C_GROUPS)

    out = jnp.concatenate([o_a, o_b, o_c], axis=-1) @ p['w_out']
    return out, s_new.astype(s0.dtype), k_b, v_b, log_f_b, vn


def hier_moe(h, p):
    B_, L, D = h.shape
    f32 = jnp.float32
    t = h.reshape(B_ * L, D)
    lg = (t @ p['router_group_w']).astype(f32) + p['router_group_b'].astype(f32)
    pg = jax.nn.softmax(lg, axis=-1)
    gsel = jnp.argmax(lg, axis=-1)
    gate_g = jnp.take_along_axis(pg, gsel[:, None], axis=-1)
    le = jnp.einsum('td,gde->tge', t, p['router_expert_w']).astype(f32) + p['router_expert_b'].astype(f32)
    le_sel = jnp.take_along_axis(le, gsel[:, None, None], axis=1)[:, 0]
    top_v, top_i = lax.top_k(le_sel, TOP_K_FINE)
    w = jax.nn.softmax(top_v, axis=-1)
    fine = jnp.sum(jax.nn.one_hot(top_i, EXPERTS_PER_GROUP, dtype=f32) * w[..., None], axis=1)
    comb = (jax.nn.one_hot(gsel, N_EXPERT_GROUPS, dtype=f32)[:, :, None]
            * (gate_g * fine)[:, None, :]).astype(h.dtype)
    y = jnp.zeros_like(t)
    for g in range(N_EXPERT_GROUPS):
        a = jnp.einsum('td,edf->tef', t, p['expert_w_gate'][g])
        up = jnp.einsum('td,edf->tef', t, p['expert_w_up'][g])
        y = y + jnp.einsum('tef,efd->td', jax.nn.silu(a) * up * comb[:, g, :, None], p['expert_w_down'][g])
    return y.reshape(B_, L, D)


def layer(x, p, lb, s0, past_k, past_v, past_logf):
    mix, s_new, k_b, v_b, lf_b, vn = token_mixers(rmsnorm(x, p['norm_mix_gain']), p, lb, s0, past_k, past_v, past_logf)
    x = x + mix
    x = x + hier_moe(rmsnorm(x, p['norm_ffn_gain']), p)
    return x, s_new, k_b, v_b, lf_b, vn


def setup_inputs(seed: int = 0) -> dict:
    key = jax.random.key(seed)
    k = jax.random.split(key, 32)

    def nrm(i, shape, scale):
        return scale * jax.random.normal(k[i], shape, jnp.float32)

    n_pages = PAST_LEN // PAGE_SIZE
    n_used = DEC_BATCH * n_pages
    n_pool = n_used + max(1, n_used // 4)
    G, E, F = N_EXPERT_GROUPS, EXPERTS_PER_GROUP, EXPERT_FF
    page_table = jax.random.permutation(k[6], n_pool)[:n_used].reshape(DEC_BATCH, n_pages).astype(jnp.int32)
    return {
        'x_prompt': nrm(0, (BATCH, SEQ, D_MODEL), 1.0),
        'x_sample': nrm(1, (DEC_BATCH, DEC_SEQ, D_MODEL), 1.0),
        'cache_k': nrm(2, (DEPTH, n_pool, PAGE_SIZE, B_HEADS, HEAD_DIM), 1.0),
        'cache_v': nrm(3, (DEPTH, n_pool, PAGE_SIZE, B_HEADS, HEAD_DIM), 1.0),
        'cache_logf': jax.nn.log_sigmoid(FOX_F_BIAS_INIT + nrm(4, (DEPTH, n_pool, PAGE_SIZE, B_HEADS), 1.0)),
        'state_hgrn': nrm(5, (DEPTH, DEC_BATCH, A_HEADS, A_KEY_DIM, A_VAL_DIM), 0.5),
        'page_table': page_table,
        'norm_mix_gain': 1.0 + nrm(7, (DEPTH, D_MODEL), 0.02),
        'w_in': nrm(8, (DEPTH, D_MODEL, IN_COLS), D_MODEL ** -0.5),
        'hgrn_lb_logits': nrm(9, (DEPTH, A_KEY_WIDTH), 0.5),
        'hgrn_out_gain': 1.0 + nrm(10, (DEPTH, A_WIDTH), 0.02),
        'fox_f_bias': FOX_F_BIAS_INIT + nrm(11, (DEPTH, B_HEADS), 0.1),
        'fox_out_gain': 1.0 + nrm(12, (DEPTH, B_WIDTH), 0.02),
        'sgu_v_gain': 1.0 + nrm(13, (DEPTH, C_WIDTH), 0.02),
        'sgu_w_s': nrm(14, (DEPTH, C_GROUPS, SGU_CHUNK, SGU_CHUNK), SGU_CHUNK ** -0.5),
        'sgu_b': 1.0 + nrm(15, (DEPTH, C_GROUPS, SGU_CHUNK), 0.02),
        'sgu_out_gain': 1.0 + nrm(16, (DEPTH, C_WIDTH), 0.02),
        'w_out': nrm(17, (DEPTH, D_MIX, D_MODEL), D_MIX ** -0.5),
        'norm_ffn_gain': 1.0 + nrm(18, (DEPTH, D_MODEL), 0.02),
        'router_group_w': nrm(19, (DEPTH, D_MODEL, G), D_MODEL ** -0.5),
        'router_group_b': nrm(20, (DEPTH, G), 0.01),
        'router_expert_w': nrm(21, (DEPTH, G, D_MODEL, E), D_MODEL ** -0.5),
        'router_expert_b': nrm(22, (DEPTH, G, E), 0.01),
        'expert_w_gate': nrm(23, (DEPTH, G, E, D_MODEL, F), D_MODEL ** -0.5),
        'expert_w_up': nrm(24, (DEPTH, G, E, D_MODEL, F), D_MODEL ** -0.5),
        'expert_w_down': nrm(25, (DEPTH, G, E, F, D_MODEL), F ** -0.5),
        'norm_final_gain': 1.0 + nrm(26, (D_MODEL,), 0.02),
    }


def reference(x_prompt, x_sample, cache_k, cache_v, cache_logf, state_hgrn, page_table,
              norm_mix_gain, w_in, hgrn_lb_logits, hgrn_out_gain, fox_f_bias, fox_out_gain,
              sgu_v_gain, sgu_w_s, sgu_b, sgu_out_gain, w_out, norm_ffn_gain,
              router_group_w, router_group_b, router_expert_w, router_expert_b,
              expert_w_gate, expert_w_up, expert_w_down, norm_final_gain):
    lb_all = jnp.cumsum(jax.nn.softmax(hgrn_lb_logits.astype(jnp.float32), axis=0), axis=0)
    lb_all = lb_all - lb_all[0]
    n_b = x_prompt.shape[0]
    n_db, n_pages = page_table.shape
    past = n_pages * PAGE_SIZE
    xp, xs = x_prompt, x_sample
    sp_l, kp_l, vp_l, lfp_l = [], [], [], []
    ss_l, ks_l, vs_l, lfs_l, vns_l = [], [], [], [], []
    for l in range(DEPTH):
        p = {
            'norm_mix_gain': norm_mix_gain[l], 'w_in': w_in[l], 'hgrn_out_gain': hgrn_out_gain[l],
            'fox_f_bias': fox_f_bias[l], 'fox_out_gain': fox_out_gain[l], 'sgu_v_gain': sgu_v_gain[l],
            'sgu_w_s': sgu_w_s[l], 'sgu_b': sgu_b[l], 'sgu_out_gain': sgu_out_gain[l], 'w_out': w_out[l],
            'norm_ffn_gain': norm_ffn_gain[l], 'router_group_w': router_group_w[l],
            'router_group_b': router_group_b[l], 'router_expert_w': router_expert_w[l],
            'router_expert_b': router_expert_b[l], 'expert_w_gate': expert_w_gate[l],
            'expert_w_up': expert_w_up[l], 'expert_w_down': expert_w_down[l],
        }
        s0p = jnp.zeros((n_b, A_HEADS, A_KEY_DIM, A_VAL_DIM), x_prompt.dtype)
        xp, sp, kp, vp, lfp, _ = layer(xp, p, lb_all[l], s0p, None, None, None)
        sp_l.append(sp); kp_l.append(kp); vp_l.append(vp); lfp_l.append(lfp)
        pk = cache_k[l, page_table].reshape(n_db, past, B_HEADS, HEAD_DIM)
        pv = cache_v[l, page_table].reshape(n_db, past, B_HEADS, HEAD_DIM)
        plf = cache_logf[l, page_table].reshape(n_db, past, B_HEADS)
        xs, ss, ks, vs, lfs, vns = layer(xs, p, lb_all[l], state_hgrn[l], pk, pv, plf)
        ss_l.append(ss); ks_l.append(ks); vs_l.append(vs); lfs_l.append(lfs); vns_l.append(vns)
    y_prompt = rmsnorm(xp, norm_final_gain)
    y_sample = rmsnorm(xs, norm_final_gain)
    return (y_prompt, y_sample,
            jnp.stack(kp_l), jnp.stack(vp_l), jnp.stack(lfp_l), jnp.stack(sp_l),
            jnp.stack(ks_l), jnp.stack(vs_l), jnp.stack(lfs_l), jnp.stack(ss_l), jnp.stack(vns_l))
```

```python
import functools

import numpy as np
import jax
import jax.numpy as jnp
from jax import lax
from jax.experimental import pallas as pl
from jax.experimental.pallas import tpu as pltpu

F32 = jnp.float32
BF16 = jnp.bfloat16

D_MODEL = 1024
HEAD_DIM = 64
A_HEADS = 4
A_WIDTH = A_HEADS * HEAD_DIM
B_HEADS = 8
B_WIDTH = B_HEADS * HEAD_DIM
C_GROUPS = 4
C_WIDTH = C_GROUPS * HEAD_DIM
CHUNK = 128
N_GROUPS = 4
N_EXP = 4
N_EXPERTS = N_GROUPS * N_EXP
EXPERT_FF = D_MODEL // 4
RMS_EPS = 1e-6
NEG_INF = -1e30
LANES = 128
IN_COLS_PAD = 3200
HGRN_LEVELS = (1, 2, 4, 8, 16, 32, 64)
VMEM_LIMIT = 48 * 1024 * 1024


def _mm(a, b):
    return jnp.dot(a, b, preferred_element_type=F32)


def _mm_nt(a, b):
    return lax.dot_general(a, b, (((1,), (1,)), ((), ())), preferred_element_type=F32)


def _split3(x):
    p1 = x.astype(BF16)
    r1 = x - p1.astype(F32)
    p2 = r1.astype(BF16)
    r2 = r1 - p2.astype(F32)
    return p1, p2, r2.astype(BF16)


def _dot01_left(m01, x):
    p1, p2, p3 = _split3(x)
    return _mm(m01, p1) + _mm(m01, p2) + _mm(m01, p3)


def _dot01_right(x, m01):
    p1, p2, p3 = _split3(x)
    return _mm(p1, m01) + _mm(p2, m01) + _mm(p3, m01)


def _sigmoid(x):
    return 1.0 / (1.0 + jnp.exp(-x))


def _log_sigmoid(x):
    return jnp.minimum(x, 0.0) - jnp.log1p(jnp.exp(-jnp.abs(x)))


def _gelu_tanh(x):
    return x * (0.5 * (1.0 + jnp.tanh(0.7978845608028654 * (x + 0.044715 * (x * x * x)))))


def _rmsnorm_rows(x, g):
    return x * lax.rsqrt(jnp.mean(x * x, axis=-1, keepdims=True) + RMS_EPS) * g


def _group_rmsnorm(x, e01, g):
    ms = _dot01_right(x * x, e01) * (1.0 / HEAD_DIM)
    return x * lax.rsqrt(ms + RMS_EPS) * g


def _block_ones(n, blk):
    i = np.arange(n)
    return (i[:, None] // blk == i[None, :] // blk).astype(np.float32)


def _hgrn_consts():
    n = CHUNK
    t = np.arange(n)[:, None]
    j = np.arange(n)[None, :]
    mats = [j <= t, j > t]
    masks = []
    for h in HGRN_LEVELS:
        same = (t // (2 * h)) == (j // (2 * h))
        off_j = j % (2 * h)
        off_t = t % (2 * h)
        mats.append(same & (off_j >= h) & (j <= t))
        mats.append(same & (off_j < h) & (j > t))
        masks.append(same & (off_t >= h) & (off_j < h))
    cm = np.concatenate([m.astype(np.float32) for m in mats], axis=0)
    mk = np.stack([m.astype(np.float32) for m in masks])
    return cm, mk


def _inproj_kernel(layer, x_ref, g_ref, w_ref, wft_ref, lbl_ref, fb_ref, fbc_ref, vg_ref,
                   smat_ref, sbias_ref, sog_ref, e_ref,
                   qa_ref, lfa_ref, ka_ref, ia_ref, ga_ref, qb_ref, kb_ref, vb_ref,
                   kb16_ref, vb16_ref, lfb_ref, lfbt_ref, oc_ref, vn_ref):
    tm = x_ref.shape[0]
    h = _rmsnorm_rows(x_ref[...], g_ref[...])
    hb = h.astype(BF16)
    y = _mm(hb, w_ref[...])
    a_q = y[:, 0:256]
    z = y[:, 256:512]
    a_i = y[:, 512:768]
    a_g = y[:, 768:1024]
    b_q = y[:, 1024:1536]
    b_k = y[:, 1536:2048]
    b_v = y[:, 2048:2560]
    c_u = y[:, 2560:2816]
    c_v = y[:, 2816:3072]
    b_f = y[:, 3072:3200]

    lbl = lbl_ref[...]
    ex = jnp.exp(lbl - jnp.max(lbl, axis=0, keepdims=True))
    p = ex / jnp.sum(ex, axis=0, keepdims=True)
    cs = p[0:1]
    for j in range(1, layer + 1):
        cs = cs + p[j:j + 1]
    lb = cs - p[0:1]
    la = jnp.log(lb)
    lbb = jnp.log1p(-lb) + _log_sigmoid(z)
    mx = jnp.maximum(la, lbb)
    lfa_ref[...] = mx + jnp.log1p(jnp.exp(-jnp.abs(la - lbb)))
    ka_ref[...] = (1.0 - lb) * _sigmoid(-z)
    qa_ref[...] = a_q * _sigmoid(a_q)
    ia_ref[...] = a_i
    ga_ref[...] = a_g * _sigmoid(a_g)

    qb_ref[...] = (b_q * (HEAD_DIM ** -0.5)).astype(BF16)
    kb_ref[...] = b_k
    vb_ref[...] = b_v
    kb16_ref[...] = b_k.astype(BF16)
    vb16_ref[...] = b_v.astype(BF16)
    lfb = _log_sigmoid(b_f + fb_ref[...])
    lfb_ref[...] = lfb[:, 0:B_HEADS]
    zt = _mm_nt(wft_ref[...], hb)
    lfbt_ref[...] = _log_sigmoid(zt + fbc_ref[...])

    e256 = e_ref[...]
    u = _gelu_tanh(c_u)
    vn = _group_rmsnorm(_gelu_tanh(c_v), e256, vg_ref[...])
    vn_ref[...] = vn
    vnb = vn.astype(BF16)
    left = lax.broadcasted_iota(jnp.int32, (CHUNK, LANES), 1) < HEAD_DIM
    zrows = []
    for c in range(tm // CHUNK):
        zp = []
        for pr in range(2):
            vp = vnb[c * CHUNK:(c + 1) * CHUNK, pr * LANES:(pr + 1) * LANES]
            zp.append(jnp.where(left, _mm(smat_ref[2 * pr], vp), _mm(smat_ref[2 * pr + 1], vp)))
        zrows.append(jnp.concatenate(zp, axis=1) + sbias_ref[...])
    zc = zrows[0] if len(zrows) == 1 else jnp.concatenate(zrows, axis=0)
    oc_ref[...] = _group_rmsnorm(u * zc, e256, sog_ref[...]).astype(BF16)


def _inproj(layer, x2d, g, w_r, wft, lbl, fb, fbc, vg, smat, sbias, sog, e256, tm):
    t = x2d.shape[0]
    grid = (t // tm,)
    row = lambda w: pl.BlockSpec((tm, w), lambda i: (i, 0))
    full = lambda a: pl.BlockSpec(a.shape, lambda i: (0,) * a.ndim)
    outs = [
        (A_WIDTH, F32), (A_WIDTH, F32), (A_WIDTH, F32), (A_WIDTH, F32), (A_WIDTH, F32),
        (B_WIDTH, BF16), (B_WIDTH, F32), (B_WIDTH, F32), (B_WIDTH, BF16), (B_WIDTH, BF16),
        (B_HEADS, F32),
    ]
    out_shape = [jax.ShapeDtypeStruct((t, w), d) for w, d in outs]
    out_specs = [row(w) for w, _ in outs]
    out_shape.append(jax.ShapeDtypeStruct((16, t), F32))
    out_specs.append(pl.BlockSpec((16, tm), lambda i: (0, i)))
    out_shape += [jax.ShapeDtypeStruct((t, C_WIDTH), BF16), jax.ShapeDtypeStruct((t, C_WIDTH), F32)]
    out_specs += [row(C_WIDTH), row(C_WIDTH)]
    consts = (g, w_r, wft, lbl, fb, fbc, vg, smat, sbias, sog, e256)
    return pl.pallas_call(
        functools.partial(_inproj_kernel, layer),
        grid=grid,
        in_specs=[row(D_MODEL)] + [full(a) for a in consts],
        out_specs=out_specs,
        out_shape=out_shape,
        compiler_params=pltpu.CompilerParams(
            dimension_semantics=("arbitrary",), vmem_limit_bytes=VMEM_LIMIT),
        name="inproj",
    )(x2d, *consts)


def _hgrn_kernel(q_ref, lf_ref, k_ref, i_ref, g_ref, st0_ref, cm_ref, mk_ref, e_ref, gain_ref,
                 o_ref, stout_ref, st_scr):
    c = pl.program_id(1)

    @pl.when(c == 0)
    def _():
        st_scr[...] = st0_ref[...]

    p1, p2, p3 = _split3(lf_ref[...])
    xs = jnp.concatenate([p1, p2, p3], axis=1)
    r = _mm(cm_ref[...], xs)
    r = r[:, 0:256] + r[:, 256:512] + r[:, 512:768]
    q = q_ref[...]
    k = k_ref[...]
    iv = i_ref[...]
    ivb = iv.astype(BF16)
    b = r[0:CHUNK]
    su = r[CHUNK:2 * CHUNK]
    qd = (q * jnp.exp(b)).astype(BF16)
    kd = (k * jnp.exp(su)).astype(BF16)

    left = lax.broadcasted_iota(jnp.int32, (CHUNK, LANES), 1) < HEAD_DIM
    amat = [None] * A_HEADS
    for lv in range(len(HGRN_LEVELS)):
        base = 2 * CHUNK * (lv + 1)
        ql = q * jnp.exp(r[base:base + CHUNK])
        kl = (k * jnp.exp(r[base + CHUNK:base + 2 * CHUNK])).astype(BF16)
        m = mk_ref[lv]
        for pr in range(2):
            qp = ql[:, pr * LANES:(pr + 1) * LANES]
            kp = kl[:, pr * LANES:(pr + 1) * LANES]
            for hh in range(2):
                qm = jnp.where(left if hh == 0 else jnp.logical_not(left), qp, 0.0).astype(BF16)
                s = m * _mm_nt(qm, kp)
                hd = 2 * pr + hh
                amat[hd] = s if amat[hd] is None else amat[hd] + s

    e256 = e_ref[...]
    o_pairs = []
    for pr in range(2):
        ip = ivb[:, pr * LANES:(pr + 1) * LANES]
        o_pairs.append(jnp.where(left, _mm(amat[2 * pr].astype(BF16), ip),
                                 _mm(amat[2 * pr + 1].astype(BF16), ip)))
    o_intra = jnp.concatenate(o_pairs, axis=1)
    o_diag = _dot01_right(q * k, e256) * iv
    st = st_scr[...]
    o = _mm_nt(qd, st.astype(BF16)) + o_intra + o_diag
    o_ref[...] = (_group_rmsnorm(o, e256, gain_ref[...]) * g_ref[...]).astype(BF16)

    upd = _mm(iv.T.astype(BF16), kd)
    st_new = st * jnp.exp(b[CHUNK - 1:CHUNK, :]) + e256.astype(F32) * upd
    st_scr[...] = st_new

    @pl.when(c == pl.num_programs(1) - 1)
    def _():
        stout_ref[...] = st_new.T


def _hgrn(qa, lfa, ka, ia, ga, st0, cm, mk, e256, gain, nb, nc):
    t = qa.shape[0]
    row = pl.BlockSpec((CHUNK, A_WIDTH), lambda b, c: (b * nc + c, 0))
    full = lambda a: pl.BlockSpec(a.shape, lambda b, c: (0,) * a.ndim)
    st_spec = pl.BlockSpec((None, A_WIDTH, A_WIDTH), lambda b, c: (b, 0, 0))
    return pl.pallas_call(
        _hgrn_kernel,
        grid=(nb, nc),
        in_specs=[row, row, row, row, row, st_spec, full(cm), full(mk), full(e256), full(gain)],
        out_specs=[row, st_spec],
        out_shape=[jax.ShapeDtypeStruct((t, A_WIDTH), BF16),
                   jax.ShapeDtypeStruct((nb, A_WIDTH, A_WIDTH), F32)],
        scratch_shapes=[pltpu.VMEM((A_WIDTH, A_WIDTH), F32)],
        compiler_params=pltpu.CompilerParams(
            dimension_semantics=("arbitrary", "arbitrary"), vmem_limit_bytes=VMEM_LIMIT),
        name="hgrn",
    )(qa, lfa, ka, ia, ga, st0, cm, mk, e256, gain)


ATT_TILE = 256


def _fox_kernel(q_ref, k_ref, v_ref, lft_ref, triu_ref, e_ref, gain_ref, o_ref,
                c_scr, qm_scr, m_scr, l_scr, acc_scr):
    i = pl.program_id(1)
    tq = ATT_TILE
    nkv = c_scr.shape[0]

    @pl.when(i == 0)
    def _():
        carry = jnp.zeros((16, 1), F32)
        for c in range(nkv):
            cc = carry + _dot01_right(lft_ref[:, c * tq:(c + 1) * tq], triu_ref[...])
            c_scr[c] = cc
            carry = cc[:, tq - 1:tq]

    left = lax.broadcasted_iota(jnp.int32, (tq, LANES), 1) < HEAD_DIM
    zero = jnp.zeros((tq, LANES), BF16)
    for pr in range(B_HEADS // 2):
        qp = q_ref[:, pr * LANES:(pr + 1) * LANES]
        qm_scr[2 * pr] = jnp.where(left, qp, zero)
        qm_scr[2 * pr + 1] = jnp.where(left, zero, qp)
    m_scr[...] = jnp.full(m_scr.shape, NEG_INF, F32)
    l_scr[...] = jnp.zeros(l_scr.shape, F32)
    acc_scr[...] = jnp.zeros(acc_scr.shape, F32)

    row_id = lax.broadcasted_iota(jnp.int32, (tq, tq), 0)
    col_id = lax.broadcasted_iota(jnp.int32, (tq, tq), 1)
    causal = col_id <= row_id

    def tile(j, masked):
        ks = pl.multiple_of(j * tq, tq)
        ct = c_scr[j]
        for pr in range(B_HEADS // 2):
            kp = k_ref[pl.ds(ks, tq), pr * LANES:(pr + 1) * LANES]
            vp = v_ref[pl.ds(ks, tq), pr * LANES:(pr + 1) * LANES]
            pv = []
            al = []
            for hh in range(2):
                hd = 2 * pr + hh
                s = _mm_nt(qm_scr[hd], kp) - ct[hd:hd + 1, :]
                if masked:
                    s = jnp.where(causal, s, NEG_INF)
                m_prev = m_scr[hd]
                m_new = jnp.maximum(m_prev, jnp.max(s, axis=-1, keepdims=True))
                alpha = jnp.exp(m_prev - m_new)
                p = jnp.exp(s - m_new)
                l_scr[hd] = alpha * l_scr[hd] + jnp.sum(p, axis=-1, keepdims=True)
                m_scr[hd] = m_new
                pv.append(_mm(p.astype(BF16), vp))
                al.append(alpha)
            acc_scr[pr] = jnp.where(left, al[0], al[1]) * acc_scr[pr] + jnp.where(left, pv[0], pv[1])

    def body(j, carry):
        tile(j, False)
        return carry

    lax.fori_loop(0, i, body, 0)
    tile(i, True)

    e128 = e_ref[0:LANES, 0:LANES]
    for pr in range(B_HEADS // 2):
        linv = jnp.where(left, 1.0 / l_scr[2 * pr], 1.0 / l_scr[2 * pr + 1])
        o = acc_scr[pr] * linv
        g = gain_ref[:, pr * LANES:(pr + 1) * LANES]
        o_ref[:, pr * LANES:(pr + 1) * LANES] = _group_rmsnorm(o, e128, g).astype(BF16)


def _fox(qb, kb16, vb16, lfbt, triu, e256, gain, nb, seq):
    t = qb.shape[0]
    tq = ATT_TILE
    nq = seq // tq
    return pl.pallas_call(
        _fox_kernel,
        grid=(nb, nq),
        in_specs=[
            pl.BlockSpec((tq, B_WIDTH), lambda b, i: (b * nq + i, 0)),
            pl.BlockSpec((seq, B_WIDTH), lambda b, i: (b, 0)),
            pl.BlockSpec((seq, B_WIDTH), lambda b, i: (b, 0)),
            pl.BlockSpec((16, seq), lambda b, i: (0, b)),
            pl.BlockSpec(triu.shape, lambda b, i: (0, 0)),
            pl.BlockSpec(e256.shape, lambda b, i: (0, 0)),
            pl.BlockSpec(gain.shape, lambda b, i: (0, 0)),
        ],
        out_specs=pl.BlockSpec((tq, B_WIDTH), lambda b, i: (b * nq + i, 0)),
        out_shape=jax.ShapeDtypeStruct((t, B_WIDTH), BF16),
        scratch_shapes=[
            pltpu.VMEM((nq, 16, tq), F32),
            pltpu.VMEM((B_HEADS, tq, LANES), BF16),
            pltpu.VMEM((B_HEADS, tq, 1), F32),
            pltpu.VMEM((B_HEADS, tq, 1), F32),
            pltpu.VMEM((B_HEADS // 2, tq, LANES), F32),
        ],
        compiler_params=pltpu.CompilerParams(
            dimension_semantics=("arbitrary", "arbitrary"), vmem_limit_bytes=VMEM_LIMIT),
        name="fox_prompt",
    )(qb, kb16, vb16, lfbt, triu, e256, gain)


def _paged_kernel(layer, pt_ref, q_ref, kc_ref, vc_ref, lfc_ref, kn_ref, vn_ref, lfn_ref,
                  triu_ref, rrep_ref, rep_ref, hm_ref, nm_ref, gain_ref, o_ref,
                  m_scr, l_scr, acc_scr, carry_scr, pad_scr):
    del layer, pt_ref
    p = pl.program_id(1)
    n_pages = pl.num_programs(1) - 1
    flat = CHUNK * B_HEADS

    @pl.when(p == 0)
    def _():
        m_scr[...] = jnp.full(m_scr.shape, NEG_INF, F32)
        l_scr[...] = jnp.zeros(l_scr.shape, F32)
        acc_scr[...] = jnp.zeros(acc_scr.shape, F32)
        carry_scr[...] = jnp.zeros(carry_scr.shape, F32)
        pad_scr[...] = jnp.zeros(pad_scr.shape, F32)

    def step(k_ref, v_ref, lf_ref, mask):
        kf = k_ref[...].reshape(flat, HEAD_DIM).astype(BF16)
        vf = v_ref[...].reshape(flat, HEAD_DIM).astype(BF16)
        pad_scr[:, 0:B_HEADS] = lf_ref[...]
        lft = pad_scr[...].T
        ct = carry_scr[...] + _dot01_right(lft, triu_ref[...])
        carry_scr[...] = jnp.broadcast_to(ct[:, CHUNK - 1:CHUNK], (CHUNK, CHUNK))
        crep = _dot01_left(rrep_ref[...], ct)
        cb = _dot01_right(crep, rep_ref[...])
        s = _mm_nt(q_ref[...], kf) - cb
        s = jnp.where(mask > 0.0, s, NEG_INF)
        m_prev = m_scr[...]
        m_new = jnp.maximum(m_prev, jnp.max(s, axis=-1, keepdims=True))
        alpha = jnp.exp(m_prev - m_new)
        pr = jnp.exp(s - m_new)
        l_scr[...] = alpha * l_scr[...] + jnp.sum(pr, axis=-1, keepdims=True)
        acc_scr[...] = alpha * acc_scr[...] + _mm(pr.astype(BF16), vf)
        m_scr[...] = m_new

    @pl.when(p < n_pages)
    def _():
        step(kc_ref, vc_ref, lfc_ref, hm_ref[...])

    @pl.when(p == n_pages)
    def _():
        step(kn_ref, vn_ref, lfn_ref, nm_ref[...])
        o = acc_scr[...] / l_scr[...]
        o_ref[...] = _rmsnorm_rows(o, gain_ref[...]).astype(o_ref.dtype)


def _paged(layer, page_table, q_r, cache_k, cache_v, cache_logf, kn, vn, lfn,
           triu, rrep, rep, hm, nm, gain_rows):
    nb, n_pages = page_table.shape
    rows = q_r.shape[1]

    def page_idx(b, p, pt):
        return pt[b, jnp.minimum(p, n_pages - 1)]

    full2 = lambda a: pl.BlockSpec(a.shape, lambda b, p, pt: (0, 0))
    grid_spec = pltpu.PrefetchScalarGridSpec(
        num_scalar_prefetch=1,
        grid=(nb, n_pages + 1),
        in_specs=[
            pl.BlockSpec((None, rows, HEAD_DIM), lambda b, p, pt: (b, 0, 0)),
            pl.BlockSpec((None, None, CHUNK, B_HEADS, HEAD_DIM),
                         lambda b, p, pt: (layer, page_idx(b, p, pt), 0, 0, 0)),
            pl.BlockSpec((None, None, CHUNK, B_HEADS, HEAD_DIM),
                         lambda b, p, pt: (layer, page_idx(b, p, pt), 0, 0, 0)),
            pl.BlockSpec((None, None, CHUNK, B_HEADS),
                         lambda b, p, pt: (layer, page_idx(b, p, pt), 0, 0)),
            pl.BlockSpec((None, CHUNK, B_HEADS, HEAD_DIM), lambda b, p, pt: (b, 0, 0, 0)),
            pl.BlockSpec((None, CHUNK, B_HEADS, HEAD_DIM), lambda b, p, pt: (b, 0, 0, 0)),
            pl.BlockSpec((None, CHUNK, B_HEADS), lambda b, p, pt: (b, 0, 0)),
            full2(triu), full2(rrep), full2(rep), full2(hm), full2(nm), full2(gain_rows),
        ],
        out_specs=pl.BlockSpec((None, rows, HEAD_DIM), lambda b, p, pt: (b, 0, 0)),
        scratch_shapes=[
            pltpu.VMEM((rows, 1), F32),
            pltpu.VMEM((rows, 1), F32),
            pltpu.VMEM((rows, HEAD_DIM), F32),
            pltpu.VMEM((CHUNK, CHUNK), F32),
            pltpu.VMEM((CHUNK, CHUNK), F32),
        ],
    )
    return pl.pallas_call(
        functools.partial(_paged_kernel, layer),
        grid_spec=grid_spec,
        out_shape=jax.ShapeDtypeStruct((nb, rows, HEAD_DIM), BF16),
        compiler_params=pltpu.CompilerParams(
            dimension_semantics=("arbitrary", "arbitrary"), vmem_limit_bytes=VMEM_LIMIT),
        name="fox_paged",
    )(page_table, q_r, cache_k, cache_v, cache_logf, kn, vn, lfn,
      triu, rrep, rep, hm, nm, gain_rows)


def _outproj_kernel(x_ref, oa_ref, ob_ref, oc_ref, w_ref, g_ref, wrt_ref, brc_ref,
                    x2_ref, h2_ref, comb_ref):
    tm = x_ref.shape[0]
    mix = (_mm(oa_ref[...], w_ref[0:A_WIDTH, :])
           + _mm(ob_ref[...], w_ref[A_WIDTH:A_WIDTH + B_WIDTH, :])
           + _mm(oc_ref[...], w_ref[A_WIDTH + B_WIDTH:, :]))
    x2 = x_ref[...] + mix
    x2_ref[...] = x2
    h2 = _rmsnorm_rows(x2, g_ref[...])
    h2b = h2.astype(BF16)
    h2_ref[...] = h2b
    h2l = (h2 - h2b.astype(F32)).astype(BF16)
    wr = wrt_ref[...]
    wrh = wr.astype(BF16)
    wrl = (wr - wrh.astype(F32)).astype(BF16)
    lg = _mm_nt(wrh, h2b) + _mm_nt(wrh, h2l) + _mm_nt(wrl, h2b) + brc_ref[...]
    row = [lg[i:i + 1, :] for i in range(N_GROUPS + N_EXPERTS)]

    g = row[0:N_GROUPS]
    gmax = jnp.maximum(jnp.maximum(g[0], g[1]), jnp.maximum(g[2], g[3]))
    gsel = jnp.where(g[0] == gmax, 0, jnp.where(g[1] == gmax, 1, jnp.where(g[2] == gmax, 2, 3)))
    den = (jnp.exp(g[0] - gmax) + jnp.exp(g[1] - gmax)) + (jnp.exp(g[2] - gmax) + jnp.exp(g[3] - gmax))
    gate = 1.0 / den
    le = []
    for e in range(N_EXP):
        le.append(jnp.where(gsel == 0, row[4 + e],
                            jnp.where(gsel == 1, row[8 + e],
                                      jnp.where(gsel == 2, row[12 + e], row[16 + e]))))
    v1 = jnp.maximum(jnp.maximum(le[0], le[1]), jnp.maximum(le[2], le[3]))
    i1 = jnp.where(le[0] == v1, 0, jnp.where(le[1] == v1, 1, jnp.where(le[2] == v1, 2, 3)))
    le2 = [jnp.where(i1 == e, -jnp.inf, le[e]) for e in range(N_EXP)]
    v2 = jnp.maximum(jnp.maximum(le2[0], le2[1]), jnp.maximum(le2[2], le2[3]))
    i2 = jnp.where(le2[0] == v2, 0, jnp.where(le2[1] == v2, 1, jnp.where(le2[2] == v2, 2, 3)))
    ex = jnp.exp(v2 - v1)
    w1 = 1.0 / (1.0 + ex)
    w2 = ex * w1
    rid = lax.broadcasted_iota(jnp.int32, (N_EXPERTS, tm), 0)
    combt = jnp.zeros((N_EXPERTS, tm), F32)
    for gi in range(N_GROUPS):
        for e in range(N_EXP):
            fine = jnp.where(i1 == e, w1, 0.0) + jnp.where(i2 == e, w2, 0.0)
            val = jnp.where(gsel == gi, gate * fine, 0.0)
            combt = jnp.where(rid == gi * N_EXP + e, val, combt)
    combt = jnp.concatenate([combt, jnp.zeros((LANES - N_EXPERTS, tm), F32)], axis=0)
    comb_ref[...] = combt.T


def _outproj(x2d, oa, ob, oc, w_out, g, wrt, brc, tm):
    t = x2d.shape[0]
    row = lambda w: pl.BlockSpec((tm, w), lambda i: (i, 0))
    full = lambda a: pl.BlockSpec(a.shape, lambda i: (0,) * a.ndim)
    return pl.pallas_call(
        _outproj_kernel,
        grid=(t // tm,),
        in_specs=[row(D_MODEL), row(A_WIDTH), row(B_WIDTH), row(C_WIDTH),
                  full(w_out), full(g), full(wrt), full(brc)],
        out_specs=[row(D_MODEL), row(D_MODEL), row(LANES)],
        out_shape=[jax.ShapeDtypeStruct((t, D_MODEL), F32),
                   jax.ShapeDtypeStruct((t, D_MODEL), BF16),
                   jax.ShapeDtypeStruct((t, LANES), F32)],
        compiler_params=pltpu.CompilerParams(
            dimension_semantics=("arbitrary",), vmem_limit_bytes=VMEM_LIMIT),
        name="outproj_router",
    )(x2d, oa, ob, oc, w_out, g, wrt, brc)


def _moe_kernel(final, x2_ref, h2_ref, comb_ref, wg_ref, wu_ref, wd_ref, gf_ref, y_ref, acc_scr):
    e = pl.program_id(1)

    @pl.when(e == 0)
    def _():
        acc_scr[...] = x2_ref[...]

    h = h2_ref[...]
    a = _mm(h, wg_ref[...])
    u = _mm(h, wu_ref[...])
    comb = comb_ref[...]
    lane = lax.broadcasted_iota(jnp.int32, comb.shape, 1)
    cw = jnp.sum(jnp.where(lane == e, comb, 0.0), axis=-1, keepdims=True)
    act = (a * _sigmoid(a)) * u * cw
    acc_scr[...] += _mm(act.astype(BF16), wd_ref[...])

    @pl.when(e == pl.num_programs(1) - 1)
    def _():
        y = acc_scr[...]
        if final:
            y = _rmsnorm_rows(y, gf_ref[...])
        y_ref[...] = y


def _moe(final, x2, h2, comb, wg, wu, wd, gf, tm):
    t = x2.shape[0]
    return pl.pallas_call(
        functools.partial(_moe_kernel, final),
        grid=(t // tm, N_EXPERTS),
        in_specs=[
            pl.BlockSpec((tm, D_MODEL), lambda i, e: (i, 0)),
            pl.BlockSpec((tm, D_MODEL), lambda i, e: (i, 0)),
            pl.BlockSpec((tm, LANES), lambda i, e: (i, 0)),
            pl.BlockSpec((None, D_MODEL, EXPERT_FF), lambda i, e: (e, 0, 0)),
            pl.BlockSpec((None, D_MODEL, EXPERT_FF), lambda i, e: (e, 0, 0)),
            pl.BlockSpec((None, EXPERT_FF, D_MODEL), lambda i, e: (e, 0, 0)),
            pl.BlockSpec(gf.shape, lambda i, e: (0, 0)),
        ],
        out_specs=pl.BlockSpec((tm, D_MODEL), lambda i, e: (i, 0)),
        out_shape=jax.ShapeDtypeStruct((t, D_MODEL), F32),
        scratch_shapes=[pltpu.VMEM((tm, D_MODEL), F32)],
        compiler_params=pltpu.CompilerParams(
            dimension_semantics=("arbitrary", "arbitrary"), vmem_limit_bytes=VMEM_LIMIT),
        name="moe",
    )(x2, h2, comb, wg, wu, wd, gf)


def _block_diag_state(s):
    n = s.shape[0]
    out = jnp.zeros((n, A_HEADS, HEAD_DIM, A_HEADS, HEAD_DIM), s.dtype)
    st = jnp.swapaxes(s, -1, -2)
    for h in range(A_HEADS):
        out = out.at[:, h, :, h, :].set(st[:, h])
    return out.reshape(n, A_WIDTH, A_WIDTH)


def _state_blocks(sbd):
    n = sbd.shape[0]
    s5 = sbd.reshape(n, A_HEADS, HEAD_DIM, A_HEADS, HEAD_DIM)
    return jnp.stack([s5[:, h, :, h, :] for h in range(A_HEADS)], axis=1)


def _pad_rows(a, nb, per, to):
    w = a.shape[-1]
    a3 = a.reshape(nb, per, w)
    return jnp.pad(a3, ((0, 0), (0, to - per), (0, 0))).reshape(nb * to, w)


def kernel(x_prompt, x_sample, cache_k, cache_v, cache_logf, state_hgrn, page_table, norm_mix_gain, w_in, hgrn_lb_logits, hgrn_out_gain, fox_f_bias, fox_out_gain, sgu_v_gain, sgu_w_s, sgu_b, sgu_out_gain, w_out, norm_ffn_gain, router_group_w, router_group_b, router_expert_w, router_expert_b, expert_w_gate, expert_w_up, expert_w_down, norm_final_gain):
    nb, seq, _ = x_prompt.shape
    ndb, dseq, _ = x_sample.shape
    depth = w_in.shape[0]
    n_pages = page_table.shape[1]
    tp = nb * seq
    ts = ndb * dseq
    assert seq % ATT_TILE == 0 and ts % 8 == 0 and ts <= CHUNK and CHUNK % dseq == 0

    cm_np, mk_np = _hgrn_consts()
    cm = jnp.asarray(cm_np, BF16)
    mk = jnp.asarray(mk_np, F32)
    e256 = jnp.asarray(_block_ones(A_WIDTH, HEAD_DIM), BF16)
    ar = np.arange(ATT_TILE)
    triu_att = jnp.asarray((ar[:, None] <= ar[None, :]).astype(np.float32), BF16)
    ac = np.arange(CHUNK)
    triu_pg = jnp.asarray((ac[:, None] <= ac[None, :]).astype(np.float32), BF16)
    rows = B_HEADS * dseq
    rr = np.arange(rows)
    rrep = jnp.asarray((rr[:, None] // dseq == ac[None, :]).astype(np.float32), BF16)
    fc = np.arange(CHUNK * B_HEADS)
    rep = jnp.asarray((ac[:, None] == fc[None, :] // B_HEADS).astype(np.float32), BF16)
    head_ok = (fc[None, :] % B_HEADS) == (rr[:, None] // dseq)
    hm = jnp.asarray(head_ok.astype(np.float32))
    pos = fc[None, :] // B_HEADS
    nm = jnp.asarray((head_ok & (pos < dseq) & (pos <= (rr[:, None] % dseq))).astype(np.float32))

    sr = np.arange(ts)
    same_seq = jnp.asarray(((sr[:, None] // dseq) == (sr[None, :] // dseq))
                           & ((sr[None, :] % dseq) <= (sr[:, None] % dseq)))
    tril = jnp.asarray(np.tril(np.ones((CHUNK, CHUNK), np.float32)))

    xp = x_prompt.reshape(tp, D_MODEL)
    xs = x_sample.reshape(ts, D_MODEL)
    zeros_state = jnp.zeros((nb, A_WIDTH, A_WIDTH), F32)

    outs_p = {k: [] for k in ("k", "v", "lf", "s")}
    outs_s = {k: [] for k in ("k", "v", "lf", "s", "vn")}

    for l in range(depth):
        wl = w_in[l]
        w_r = jnp.concatenate(
            [wl[:, 0:2560], wl[:, 2568:3080], wl[:, 2560:2568],
             jnp.zeros((D_MODEL, IN_COLS_PAD - 3080), F32)], axis=1).astype(BF16)
        wft = jnp.pad(wl[:, 2560:2568].T, ((0, 8), (0, 0))).astype(BF16)
        fb = jnp.pad(fox_f_bias[l][None, :], ((0, 0), (0, LANES - B_HEADS)))
        fbc = jnp.pad(fox_f_bias[l][:, None], ((0, 8), (0, 0)))
        g_mix = norm_mix_gain[l][None, :]
        vg = sgu_v_gain[l][None, :]
        sog = sgu_out_gain[l][None, :]
        smat_p = (sgu_w_s[l] * tril).astype(BF16)
        sbias_p = jnp.repeat(sgu_b[l].T, HEAD_DIM, axis=1)
        w4 = sgu_w_s[l][:, :dseq, :dseq]
        smat_s = jnp.where(same_seq, jnp.tile(w4, (1, ndb, ndb)), 0.0).astype(BF16)
        sbias_s = jnp.repeat(jnp.tile(sgu_b[l][:, :dseq].T, (ndb, 1)), HEAD_DIM, axis=1)
        hg = hgrn_out_gain[l][None, :]
        fg = fox_out_gain[l][None, :]
        fg_rows = jnp.repeat(fox_out_gain[l].reshape(B_HEADS, HEAD_DIM), dseq, axis=0)
        wo = w_out[l].astype(BF16)
        g_ffn = norm_ffn_gain[l][None, :]
        wrt = jnp.concatenate(
            [router_group_w[l].T,
             jnp.transpose(router_expert_w[l], (0, 2, 1)).reshape(N_EXPERTS, D_MODEL),
             jnp.zeros((32 - N_GROUPS - N_EXPERTS, D_MODEL), F32)], axis=0)
        brc = jnp.concatenate(
            [router_group_b[l], router_expert_b[l].reshape(-1),
             jnp.zeros((32 - N_GROUPS - N_EXPERTS,), F32)])[:, None]
        wg = expert_w_gate[l].reshape(N_EXPERTS, D_MODEL, EXPERT_FF).astype(BF16)
        wu = expert_w_up[l].reshape(N_EXPERTS, D_MODEL, EXPERT_FF).astype(BF16)
        wd = expert_w_down[l].reshape(N_EXPERTS, EXPERT_FF, D_MODEL).astype(BF16)
        gf = norm_final_gain[None, :]
        final = l == depth - 1

        (qa, lfa, ka, ia, ga, qb, kb, vb, kb16, vb16, lfb, lfbt, oc, _) = _inproj(
            l, xp, g_mix, w_r, wft, hgrn_lb_logits, fb, fbc, vg, smat_p, sbias_p, sog, e256, 256)
        oa, st = _hgrn(qa, lfa, ka, ia, ga, zeros_state, cm, mk, e256, hg, nb, seq // CHUNK)
        ob = _fox(qb, kb16, vb16, lfbt, triu_att, e256, fg, nb, seq)
        x2, h2, comb = _outproj(xp, oa, ob, oc, wo, g_ffn, wrt, brc, 256)
        xp = _moe(final, x2, h2, comb, wg, wu, wd, gf, 1024 if tp % 1024 == 0 else 256)
        outs_p["k"].append(kb.reshape(nb, seq, B_HEADS, HEAD_DIM))
        outs_p["v"].append(vb.reshape(nb, seq, B_HEADS, HEAD_DIM))
        outs_p["lf"].append(lfb.reshape(nb, seq, B_HEADS))
        outs_p["s"].append(_state_blocks(st))

        (qa, lfa, ka, ia, ga, qb, kb, vb, _, _, lfb, _, oc, vn) = _inproj(
            l, xs, g_mix, w_r, wft, hgrn_lb_logits, fb, fbc, vg, smat_s, sbias_s, sog, e256, ts)
        pads = [_pad_rows(a, ndb, dseq, CHUNK) for a in (qa, lfa, ka, ia, ga)]
        oa_pad, st = _hgrn(*pads, _block_diag_state(state_hgrn[l]), cm, mk, e256, hg, ndb, 1)
        oa = oa_pad.reshape(ndb, CHUNK, A_WIDTH)[:, :dseq].reshape(ts, A_WIDTH)
        q_r = jnp.transpose(qb.reshape(ndb, dseq, B_HEADS, HEAD_DIM), (0, 2, 1, 3)).reshape(
            ndb, rows, HEAD_DIM)
        k5 = kb.reshape(ndb, dseq, B_HEADS, HEAD_DIM)
        v5 = vb.reshape(ndb, dseq, B_HEADS, HEAD_DIM)
        padn = ((0, 0), (0, CHUNK - dseq), (0, 0), (0, 0))
        ob_r = _paged(l, page_table, q_r, cache_k, cache_v, cache_logf,
                      jnp.pad(k5, padn), jnp.pad(v5, padn),
                      jnp.pad(lfb.reshape(ndb, dseq, B_HEADS), padn[:3]),
                      triu_pg, rrep, rep, hm, nm, fg_rows)
        ob = jnp.transpose(ob_r.reshape(ndb, B_HEADS, dseq, HEAD_DIM), (0, 2, 1, 3)).reshape(
            ts, B_WIDTH)
        x2, h2, comb = _outproj(xs, oa, ob, oc, wo, g_ffn, wrt, brc, ts)
        xs = _moe(final, x2, h2, comb, wg, wu, wd, gf, ts)
        outs_s["k"].append(k5)
        outs_s["v"].append(v5)
        outs_s["lf"].append(lfb.reshape(ndb, dseq, B_HEADS))
        outs_s["s"].append(_state_blocks(st))
        outs_s["vn"].append(vn.reshape(ndb, dseq, C_WIDTH))

    return (xp.reshape(nb, seq, D_MODEL), xs.reshape(ndb, dseq, D_MODEL),
            jnp.stack(outs_p["k"]), jnp.stack(outs_p["v"]), jnp.stack(outs_p["lf"]),
            jnp.stack(outs_p["s"]),
            jnp.stack(outs_s["k"]), jnp.stack(outs_s["v"]), jnp.stack(outs_s["lf"]),
            jnp.stack(outs_s["s"]), jnp.stack(outs_s["vn"]))
```

```python
import functools

import numpy as np
import jax
import jax.numpy as jnp
from jax import lax
from jax.experimental import pallas as pl
from jax.experimental.pallas import tpu as pltpu

F32 = jnp.float32
BF16 = jnp.bfloat16

D_MODEL = 1024
HEAD_DIM = 64
A_HEADS = 4
A_WIDTH = A_HEADS * HEAD_DIM
B_HEADS = 8
B_WIDTH = B_HEADS * HEAD_DIM
C_GROUPS = 4
C_WIDTH = C_GROUPS * HEAD_DIM
CHUNK = 128
N_GROUPS = 4
N_EXP = 4
N_EXPERTS = N_GROUPS * N_EXP
EXPERT_FF = D_MODEL // 4
RMS_EPS = 1e-6
NEG_INF = -1e30
LANES = 128
SUBLANES = 8
IN_COLS_PAD = 3200
AUG_WIDTH = B_HEADS * LANES
HGRN_LEVELS = (1, 2, 4, 8, 16, 32, 64)
ATT_TQ = 256
ATT_TK = 512
PAGES_PER_STEP = 8
VMEM_LIMIT = 48 * 1024 * 1024


class _Mx:
    def __init__(self, precise):
        self.precise = precise
        self.prec = lax.Precision.HIGHEST if precise else None

    def cast(self, x):
        return x.astype(F32) if self.precise else x.astype(BF16)

    def mm(self, a, b):
        return jnp.dot(self.cast(a), self.cast(b), preferred_element_type=F32, precision=self.prec)

    def mm_nt(self, a, b):
        return lax.dot_general(self.cast(a), self.cast(b), (((1,), (1,)), ((), ())),
                               preferred_element_type=F32, precision=self.prec)


def _mm(a, b):
    return jnp.dot(a, b, preferred_element_type=F32)


def _mm_nt(a, b):
    return lax.dot_general(a, b, (((1,), (1,)), ((), ())), preferred_element_type=F32)


def _split3(x):
    p1 = x.astype(BF16)
    r1 = x - p1.astype(F32)
    p2 = r1.astype(BF16)
    r2 = r1 - p2.astype(F32)
    return p1, p2, r2.astype(BF16)


def _dot01_left(m01, x):
    p1, p2, p3 = _split3(x)
    return _mm(m01, p1) + _mm(m01, p2) + _mm(m01, p3)


def _dot01_right(x, m01):
    p1, p2, p3 = _split3(x)
    return _mm(p1, m01) + _mm(p2, m01) + _mm(p3, m01)


def _sigmoid(x):
    return 1.0 / (1.0 + jnp.exp(-x))


def _log_sigmoid(x):
    return jnp.minimum(x, 0.0) - jnp.log1p(jnp.exp(-jnp.abs(x)))


def _gelu_tanh(x):
    return x * (0.5 * (1.0 + jnp.tanh(0.7978845608028654 * (x + 0.044715 * (x * x * x)))))


def _rmsnorm_rows(x, g):
    return x * lax.rsqrt(jnp.mean(x * x, axis=-1, keepdims=True) + RMS_EPS) * g


def _group_rmsnorm(x, e01, g):
    ms = _dot01_right(x * x, e01) * (1.0 / HEAD_DIM)
    return x * lax.rsqrt(ms + RMS_EPS) * g


def _head_blocks(x):
    rows = x.shape[0]
    left = lax.broadcasted_iota(jnp.int32, (rows, LANES), 1) < HEAD_DIM
    out = []
    for pr in range(B_HEADS // 2):
        xp = x[:, pr * LANES:(pr + 1) * LANES]
        out.append(jnp.where(left, xp, 0.0))
        out.append(jnp.where(left, pltpu.roll(xp, HEAD_DIM, 1), 0.0))
    return jnp.concatenate(out, axis=1)


def _block_ones(n, blk):
    i = np.arange(n)
    return (i[:, None] // blk == i[None, :] // blk).astype(np.float32)


def _hgrn_consts():
    n = CHUNK
    t = np.arange(n)[:, None]
    j = np.arange(n)[None, :]
    mats = [j <= t, j > t]
    masks = []
    for h in HGRN_LEVELS:
        same = (t // (2 * h)) == (j // (2 * h))
        off_j = j % (2 * h)
        off_t = t % (2 * h)
        mats.append(same & (off_j >= h) & (j <= t))
        mats.append(same & (off_j < h) & (j > t))
        masks.append(same & (off_t >= h) & (off_j < h))
    cm = np.concatenate([m.astype(np.float32) for m in mats], axis=0)
    mk = np.stack([m.astype(np.float32) for m in masks])
    return cm, mk


def _aug_consts():
    place = np.zeros((3 * LANES, AUG_WIDTH), np.float32)
    qone = np.zeros((1, AUG_WIDTH), np.float32)
    vone = np.zeros((1, AUG_WIDTH), np.float32)
    for h in range(B_HEADS):
        for part in range(3):
            place[part * LANES + h, h * LANES + HEAD_DIM + part] = 1.0
            qone[0, h * LANES + HEAD_DIM + part] = 1.0
        vone[0, h * LANES + HEAD_DIM:(h + 1) * LANES] = 1.0
    return place, qone, vone


def _inproj_kernel(layer, prompt, tiles_per_seq, precise,
                   x_ref, g_ref, w_ref, lbl_ref, fb_ref, vg_ref, smat_ref, sbias_ref, sog_ref,
                   e_ref, tril_ref, place_ref, qone_ref, vone_ref,
                   qa_ref, lfa_ref, ka_ref, ia_ref, ga_ref, qb_ref, kb_ref, vb_ref, lfb_ref,
                   oc_ref, vn_ref, qaug_ref, kaug_ref, vaug_ref, carry_scr):
    mx = _Mx(precise)
    tm = x_ref.shape[0]
    h = _rmsnorm_rows(x_ref[...], g_ref[...])
    y = mx.mm(h, w_ref[...])
    a_q = y[:, 0:256]
    z = y[:, 256:512]
    a_i = y[:, 512:768]
    a_g = y[:, 768:1024]
    b_q = y[:, 1024:1536] * (HEAD_DIM ** -0.5)
    b_k = y[:, 1536:2048]
    b_v = y[:, 2048:2560]
    c_u = y[:, 2560:2816]
    c_v = y[:, 2816:3072]
    b_f = y[:, 3072:3200]

    lbl = lbl_ref[...]
    ex = jnp.exp(lbl - jnp.max(lbl, axis=0, keepdims=True))
    p = ex / jnp.sum(ex, axis=0, keepdims=True)
    cs = p[0:1]
    for j in range(1, layer + 1):
        cs = cs + p[j:j + 1]
    lb = cs - p[0:1]
    la = jnp.log(lb)
    lbb = jnp.log1p(-lb) + _log_sigmoid(z)
    lfa_ref[...] = jnp.maximum(la, lbb) + jnp.log1p(jnp.exp(-jnp.abs(la - lbb)))
    ka_ref[...] = (1.0 - lb) * _sigmoid(-z)
    qa_ref[...] = a_q * _sigmoid(a_q)
    ia_ref[...] = a_i
    ga_ref[...] = a_g * _sigmoid(a_g)

    qb_ref[...] = b_q
    kb_ref[...] = b_k
    vb_ref[...] = b_v
    lfb = _log_sigmoid(b_f + fb_ref[...])
    lfb_ref[...] = lfb[:, 0:B_HEADS]

    if prompt:
        i = pl.program_id(0)

        @pl.when(i % tiles_per_seq == 0)
        def _():
            carry_scr[...] = jnp.zeros(carry_scr.shape, F32)

        c = carry_scr[0:1, :] + _dot01_left(tril_ref[...], lfb)
        carry_scr[...] = jnp.broadcast_to(c[tm - 1:tm, :], carry_scr.shape)
        n1, n2, n3 = _split3(-c)
        bias = _mm(jnp.concatenate([n1, n2, n3], axis=1), place_ref[...])
        qaug_ref[...] = (_head_blocks(b_q) + qone_ref[...]).astype(BF16)
        kaug_ref[...] = (_head_blocks(b_k) + bias).astype(BF16)
        vaug_ref[...] = (_head_blocks(b_v) + vone_ref[...]).astype(BF16)
    else:
        qaug_ref[...] = jnp.zeros(qaug_ref.shape, BF16)
        kaug_ref[...] = jnp.zeros(kaug_ref.shape, BF16)
        vaug_ref[...] = jnp.zeros(vaug_ref.shape, BF16)

    e256 = e_ref[...]
    u = _gelu_tanh(c_u)
    vn = _group_rmsnorm(_gelu_tanh(c_v), e256, vg_ref[...])
    vn_ref[...] = vn
    left = lax.broadcasted_iota(jnp.int32, (CHUNK, LANES), 1) < HEAD_DIM
    zrows = []
    for c in range(tm // CHUNK):
        zp = []
        for pr in range(2):
            vp = vn[c * CHUNK:(c + 1) * CHUNK, pr * LANES:(pr + 1) * LANES]
            zp.append(jnp.where(left, mx.mm(smat_ref[2 * pr], vp), mx.mm(smat_ref[2 * pr + 1], vp)))
        zrows.append(jnp.concatenate(zp, axis=1) + sbias_ref[...])
    zc = zrows[0] if len(zrows) == 1 else jnp.concatenate(zrows, axis=0)
    oc_ref[...] = _group_rmsnorm(u * zc, e256, sog_ref[...])


def _inproj(layer, prompt, tiles_per_seq, precise, x2d, consts, tm):
    t = x2d.shape[0]
    row = lambda w: pl.BlockSpec((tm, w), lambda i: (i, 0))
    full = lambda a: pl.BlockSpec(a.shape, lambda i: (0,) * a.ndim)
    aug_rows = tm if prompt else SUBLANES
    aug_t = t if prompt else SUBLANES
    aug_spec = pl.BlockSpec((aug_rows, AUG_WIDTH), (lambda i: (i, 0)) if prompt else (lambda i: (0, 0)))
    outs = [(A_WIDTH, F32)] * 5 + [(B_WIDTH, F32)] * 3 + [(B_HEADS, F32), (C_WIDTH, F32), (C_WIDTH, F32)]
    out_shape = [jax.ShapeDtypeStruct((t, w), d) for w, d in outs]
    out_specs = [row(w) for w, _ in outs]
    out_shape += [jax.ShapeDtypeStruct((aug_t, AUG_WIDTH), BF16)] * 3
    out_specs += [aug_spec] * 3
    return pl.pallas_call(
        functools.partial(_inproj_kernel, layer, prompt, tiles_per_seq, precise),
        grid=(t // tm,),
        in_specs=[row(D_MODEL)] + [full(a) for a in consts],
        out_specs=out_specs,
        out_shape=out_shape,
        scratch_shapes=[pltpu.VMEM((SUBLANES, LANES), F32)],
        compiler_params=pltpu.CompilerParams(
            dimension_semantics=("arbitrary",), vmem_limit_bytes=VMEM_LIMIT),
        name="inproj",
    )(x2d, *consts)


def _hgrn_kernel(precise, q_ref, lf_ref, k_ref, i_ref, g_ref, st0_ref, cm_ref, mk_ref, e_ref,
                 gain_ref, o_ref, stout_ref, st_scr):
    mx = _Mx(precise)
    c = pl.program_id(1)

    @pl.when(c == 0)
    def _():
        st_scr[...] = st0_ref[...]

    p1, p2, p3 = _split3(lf_ref[...])
    xs = jnp.concatenate([p1, p2, p3], axis=1)
    r = _mm(cm_ref[...], xs)
    r = r[:, 0:256] + r[:, 256:512] + r[:, 512:768]
    q = q_ref[...]
    k = k_ref[...]
    iv = i_ref[...]
    b = r[0:CHUNK]
    su = r[CHUNK:2 * CHUNK]
    qd = q * jnp.exp(b)
    kd = k * jnp.exp(su)

    left = lax.broadcasted_iota(jnp.int32, (CHUNK, LANES), 1) < HEAD_DIM
    amat = [None] * A_HEADS
    for lv in range(len(HGRN_LEVELS)):
        base = 2 * CHUNK * (lv + 1)
        ql = q * jnp.exp(r[base:base + CHUNK])
        kl = k * jnp.exp(r[base + CHUNK:base + 2 * CHUNK])
        m = mk_ref[lv]
        for pr in range(2):
            qp = ql[:, pr * LANES:(pr + 1) * LANES]
            kp = kl[:, pr * LANES:(pr + 1) * LANES]
            for hh in range(2):
                qm = jnp.where(left if hh == 0 else jnp.logical_not(left), qp, 0.0)
                s = m * mx.mm_nt(qm, kp)
                hd = 2 * pr + hh
                amat[hd] = s if amat[hd] is None else amat[hd] + s

    e256 = e_ref[...]
    o_pairs = []
    for pr in range(2):
        ip = iv[:, pr * LANES:(pr + 1) * LANES]
        o_pairs.append(jnp.where(left, mx.mm(amat[2 * pr], ip), mx.mm(amat[2 * pr + 1], ip)))
    o_intra = jnp.concatenate(o_pairs, axis=1)
    o_diag = _dot01_right(q * k, e256) * iv
    st = st_scr[...]
    o = mx.mm_nt(qd, st) + o_intra + o_diag
    o_ref[...] = _group_rmsnorm(o, e256, gain_ref[...]) * g_ref[...]

    upd = mx.mm(iv.T, kd)
    st_new = st * jnp.exp(b[CHUNK - 1:CHUNK, :]) + e256.astype(F32) * upd
    st_scr[...] = st_new

    @pl.when(c == pl.num_programs(1) - 1)
    def _():
        stout_ref[...] = st_new.T


def _hgrn(precise, qa, lfa, ka, ia, ga, st0, cm, mk, e256, gain, nb, nc):
    t = qa.shape[0]
    row = pl.BlockSpec((CHUNK, A_WIDTH), lambda b, c: (b * nc + c, 0))
    full = lambda a: pl.BlockSpec(a.shape, lambda b, c: (0,) * a.ndim)
    st_spec = pl.BlockSpec((None, A_WIDTH, A_WIDTH), lambda b, c: (b, 0, 0))
    return pl.pallas_call(
        functools.partial(_hgrn_kernel, precise),
        grid=(nb, nc),
        in_specs=[row, row, row, row, row, st_spec, full(cm), full(mk), full(e256), full(gain)],
        out_specs=[row, st_spec],
        out_shape=[jax.ShapeDtypeStruct((t, A_WIDTH), F32),
                   jax.ShapeDtypeStruct((nb, A_WIDTH, A_WIDTH), F32)],
        scratch_shapes=[pltpu.VMEM((A_WIDTH, A_WIDTH), F32)],
        compiler_params=pltpu.CompilerParams(
            dimension_semantics=("arbitrary", "arbitrary"), vmem_limit_bytes=VMEM_LIMIT),
        name="hgrn",
    )(qa, lfa, ka, ia, ga, st0, cm, mk, e256, gain)


def _fox_kernel(q_ref, k_ref, v_ref, gain_ref, o_ref, m_scr, acc_scr):
    i = pl.program_id(1)
    tq, tk = ATT_TQ, ATT_TK
    m_scr[...] = jnp.full(m_scr.shape, NEG_INF, F32)
    acc_scr[...] = jnp.zeros(acc_scr.shape, F32)
    jd = (i * tq) // tk
    row_id = lax.broadcasted_iota(jnp.int32, (tq, tk), 0) + (i * tq - jd * tk)
    col_id = lax.broadcasted_iota(jnp.int32, (tq, tk), 1)
    causal = col_id <= row_id

    def tile(j, masked):
        ks = pl.multiple_of(j * tk, tk)
        for hd in range(B_HEADS):
            lanes = slice(hd * LANES, (hd + 1) * LANES)
            s = _mm_nt(q_ref[:, lanes], k_ref[pl.ds(ks, tk), lanes])
            if masked:
                s = jnp.where(causal, s, NEG_INF)
            m_prev = m_scr[hd]
            m_new = jnp.maximum(m_prev, jnp.max(s, axis=-1, keepdims=True))
            alpha = jnp.exp(m_prev - m_new)
            p = jnp.exp(s - jnp.concatenate([m_new] * (tk // LANES), axis=1))
            acc_scr[hd] = alpha * acc_scr[hd] + _mm(p.astype(BF16), v_ref[pl.ds(ks, tk), lanes])
            m_scr[hd] = m_new

    def body(j, carry):
        tile(j, False)
        return carry

    lax.fori_loop(0, jd, body, 0)
    tile(jd, True)

    left = lax.broadcasted_iota(jnp.int32, (tq, LANES), 1) < HEAD_DIM
    for pr in range(B_HEADS // 2):
        on = []
        for hh in range(2):
            a = acc_scr[2 * pr + hh]
            o = jnp.where(left, a * pltpu.roll(1.0 / a, HEAD_DIM, 1), 0.0)
            ms = jnp.sum(o * o, axis=-1, keepdims=True) * (1.0 / HEAD_DIM)
            on.append(o * lax.rsqrt(ms + RMS_EPS))
        pair = jnp.where(left, on[0], pltpu.roll(on[1], HEAD_DIM, 1))
        o_ref[:, pr * LANES:(pr + 1) * LANES] = pair * gain_ref[:, pr * LANES:(pr + 1) * LANES]


def _fox(qaug, kaug, vaug, gain, nb, seq):
    t = qaug.shape[0]
    nq = seq // ATT_TQ
    resident = lambda: pl.BlockSpec((seq, AUG_WIDTH), lambda b, i: (b, 0), pipeline_mode=pl.Buffered(1))
    return pl.pallas_call(
        _fox_kernel,
        grid=(nb, nq),
        in_specs=[
            pl.BlockSpec((ATT_TQ, AUG_WIDTH), lambda b, i: (b * nq + i, 0)),
            resident(), resident(),
            pl.BlockSpec(gain.shape, lambda b, i: (0, 0)),
        ],
        out_specs=pl.BlockSpec((ATT_TQ, B_WIDTH), lambda b, i: (b * nq + i, 0)),
        out_shape=jax.ShapeDtypeStruct((t, B_WIDTH), F32),
        scratch_shapes=[
            pltpu.VMEM((B_HEADS, ATT_TQ, LANES), F32),
            pltpu.VMEM((B_HEADS, ATT_TQ, LANES), F32),
        ],
        compiler_params=pltpu.CompilerParams(
            dimension_semantics=("arbitrary", "arbitrary"), vmem_limit_bytes=VMEM_LIMIT),
        name="fox_prompt",
    )(qaug, kaug, vaug, gain)


def _split2(x):
    hi = x.astype(BF16)
    return hi, (x - hi.astype(F32)).astype(BF16)


def _paged_kernel(n_in, pt_ref, q_ref, *refs):
    del pt_ref
    g_pages = n_in
    k_refs = refs[0:g_pages]
    v_refs = refs[g_pages:2 * g_pages]
    lf_refs = refs[2 * g_pages:3 * g_pages]
    (kn_ref, vn_ref, lfn_ref, triu_ref, bm_ref, nm_ref, gain_ref, o_ref,
     m_scr, l_scr, acc_scr, carry_scr) = refs[3 * g_pages:]
    p = pl.program_id(1)
    last = pl.num_programs(1) - 1
    rows = q_ref.shape[0]
    flat = B_HEADS * HEAD_DIM

    @pl.when(p == 0)
    def _():
        m_scr[...] = jnp.full(m_scr.shape, NEG_INF, F32)
        l_scr[...] = jnp.zeros(l_scr.shape, F32)
        acc_scr[...] = jnp.zeros(acc_scr.shape, F32)
        carry_scr[...] = jnp.zeros(carry_scr.shape, F32)

    qh, ql = _split2(q_ref[...])
    q2 = jnp.concatenate([qh, ql], axis=0)

    def step(pages, mask):
        carry = carry_scr[...]
        ss = []
        for k_ref, _, lf_ref in pages:
            kh, kl = _split2(k_ref[...].reshape(flat, CHUNK))
            lf = jnp.concatenate([lf_ref[...], jnp.zeros((SUBLANES, CHUNK), F32)], axis=0)
            c = carry + _dot01_right(lf, triu_ref[...])[0:B_HEADS]
            carry = jnp.broadcast_to(c[:, CHUNK - 1:CHUNK], carry.shape)
            sk = _mm(q2, kh)
            s = sk[0:rows] + sk[rows:2 * rows] + _mm(qh, kl)
            s = (s.reshape(B_HEADS, SUBLANES, CHUNK) - c[:, None, :]).reshape(rows, CHUNK)
            if mask is not None:
                s = jnp.where(mask > 0.0, s, NEG_INF)
            ss.append(s)
        carry_scr[...] = carry
        s_all = ss[0] if len(ss) == 1 else jnp.concatenate(ss, axis=1)
        m_prev = m_scr[...]
        m_new = jnp.maximum(m_prev, jnp.max(s_all, axis=-1, keepdims=True))
        alpha = jnp.exp(m_prev - m_new)
        p_all = jnp.exp(s_all - jnp.concatenate([m_new] * len(ss), axis=1))
        l_scr[...] = alpha * l_scr[...] + jnp.sum(p_all, axis=-1, keepdims=True)
        m_scr[...] = m_new
        pv = None
        for g, (_, v_ref, _) in enumerate(pages):
            vh, vl = _split2(v_ref[...].reshape(flat, CHUNK))
            ph, plo = _split2(p_all[:, g * CHUNK:(g + 1) * CHUNK])
            pk = _mm_nt(jnp.concatenate([ph, plo], axis=0), vh)
            d = pk[0:rows] + pk[rows:2 * rows] + _mm_nt(ph, vl)
            pv = d if pv is None else pv + d
        acc_scr[...] = jnp.concatenate([alpha] * (flat // LANES), axis=1) * acc_scr[...] + pv

    @pl.when(p < last)
    def _():
        step(list(zip(k_refs, v_refs, lf_refs)), None)

    @pl.when(p == last)
    def _():
        step([(kn_ref, vn_ref, lfn_ref)], nm_ref[...])
        o = acc_scr[...] / jnp.concatenate([l_scr[...]] * (flat // LANES), axis=1)
        o = o * bm_ref[...]
        ms = jnp.sum(o * o, axis=-1, keepdims=True) * (1.0 / HEAD_DIM)
        on = o * lax.rsqrt(ms + RMS_EPS) * gain_ref[...]
        o_ref[...] = jnp.sum(on.reshape(B_HEADS, SUBLANES, flat), axis=0)


def _paged(layer, page_table, q_bd, ck, cv, clf, kn, vn, lfn, triu, bm, nm, gain):
    nb, n_pages = page_table.shape
    rows = q_bd.shape[1]
    g_pages = min(PAGES_PER_STEP, n_pages)
    assert n_pages % g_pages == 0
    n_steps = n_pages // g_pages

    def page_spec(g, minor):
        def idx(b, p, pt):
            return (layer, pt[b, jnp.minimum(p, n_steps - 1) * g_pages + g]) + (0,) * len(minor)
        return pl.BlockSpec((None, None) + minor, idx)

    kv_minor = (B_HEADS, HEAD_DIM, CHUNK)
    full2 = lambda a: pl.BlockSpec(a.shape, lambda b, p, pt: (0, 0))
    in_specs = [pl.BlockSpec((None, rows, B_WIDTH), lambda b, p, pt: (b, 0, 0))]
    in_specs += [page_spec(g, kv_minor) for g in range(g_pages)]
    in_specs += [page_spec(g, kv_minor) for g in range(g_pages)]
    in_specs += [page_spec(g, (B_HEADS, CHUNK)) for g in range(g_pages)]
    in_specs += [
        pl.BlockSpec((None,) + kv_minor, lambda b, p, pt: (b, 0, 0, 0)),
        pl.BlockSpec((None,) + kv_minor, lambda b, p, pt: (b, 0, 0, 0)),
        pl.BlockSpec((None, B_HEADS, CHUNK), lambda b, p, pt: (b, 0, 0)),
        full2(triu), full2(bm), full2(nm), full2(gain),
    ]
    grid_spec = pltpu.PrefetchScalarGridSpec(
        num_scalar_prefetch=1,
        grid=(nb, n_steps + 1),
        in_specs=in_specs,
        out_specs=pl.BlockSpec((None, SUBLANES, B_WIDTH), lambda b, p, pt: (b, 0, 0)),
        scratch_shapes=[
            pltpu.VMEM((rows, LANES), F32),
            pltpu.VMEM((rows, LANES), F32),
            pltpu.VMEM((rows, B_WIDTH), F32),
            pltpu.VMEM((B_HEADS, CHUNK), F32),
        ],
    )
    return pl.pallas_call(
        functools.partial(_paged_kernel, g_pages),
        grid_spec=grid_spec,
        out_shape=jax.ShapeDtypeStruct((nb, SUBLANES, B_WIDTH), F32),
        compiler_params=pltpu.CompilerParams(
            dimension_semantics=("arbitrary", "arbitrary"), vmem_limit_bytes=VMEM_LIMIT),
        name="fox_paged",
    )(page_table, q_bd, *([ck] * g_pages), *([cv] * g_pages), *([clf] * g_pages),
      kn, vn, lfn, triu, bm, nm, gain)


def _outproj_kernel(precise, x_ref, oa_ref, ob_ref, oc_ref, w_ref, g_ref, wrt_ref, brc_ref,
                    x2_ref, h2_ref, comb_ref):
    mx = _Mx(precise)
    tm = x_ref.shape[0]
    mix = (mx.mm(oa_ref[...], w_ref[0:A_WIDTH, :])
           + mx.mm(ob_ref[...], w_ref[A_WIDTH:A_WIDTH + B_WIDTH, :])
           + mx.mm(oc_ref[...], w_ref[A_WIDTH + B_WIDTH:, :]))
    x2 = x_ref[...] + mix
    x2_ref[...] = x2
    h2 = _rmsnorm_rows(x2, g_ref[...])
    h2b = h2.astype(BF16)
    h2_ref[...] = h2b
    h2l = (h2 - h2b.astype(F32)).astype(BF16)
    wr = wrt_ref[...]
    wrh = wr.astype(BF16)
    wrl = (wr - wrh.astype(F32)).astype(BF16)
    lg = _mm_nt(wrh, h2b) + _mm_nt(wrh, h2l) + _mm_nt(wrl, h2b) + brc_ref[...]
    row = [lg[i:i + 1, :] for i in range(N_GROUPS + N_EXPERTS)]

    g = row[0:N_GROUPS]
    gmax = jnp.maximum(jnp.maximum(g[0], g[1]), jnp.maximum(g[2], g[3]))
    gsel = jnp.where(g[0] == gmax, 0, jnp.where(g[1] == gmax, 1, jnp.where(g[2] == gmax, 2, 3)))
    den = (jnp.exp(g[0] - gmax) + jnp.exp(g[1] - gmax)) + (jnp.exp(g[2] - gmax) + jnp.exp(g[3] - gmax))
    gate = 1.0 / den
    le = []
    for e in range(N_EXP):
        le.append(jnp.where(gsel == 0, row[4 + e],
                            jnp.where(gsel == 1, row[8 + e],
                                      jnp.where(gsel == 2, row[12 + e], row[16 + e]))))
    v1 = jnp.maximum(jnp.maximum(le[0], le[1]), jnp.maximum(le[2], le[3]))
    i1 = jnp.where(le[0] == v1, 0, jnp.where(le[1] == v1, 1, jnp.where(le[2] == v1, 2, 3)))
    le2 = [jnp.where(i1 == e, -jnp.inf, le[e]) for e in range(N_EXP)]
    v2 = jnp.maximum(jnp.maximum(le2[0], le2[1]), jnp.maximum(le2[2], le2[3]))
    i2 = jnp.where(le2[0] == v2, 0, jnp.where(le2[1] == v2, 1, jnp.where(le2[2] == v2, 2, 3)))
    ex = jnp.exp(v2 - v1)
    w1 = 1.0 / (1.0 + ex)
    w2 = ex * w1
    rid = lax.broadcasted_iota(jnp.int32, (N_EXPERTS, tm), 0)
    combt = jnp.zeros((N_EXPERTS, tm), F32)
    for gi in range(N_GROUPS):
        for e in range(N_EXP):
            fine = jnp.where(i1 == e, w1, 0.0) + jnp.where(i2 == e, w2, 0.0)
            val = jnp.where(gsel == gi, gate * fine, 0.0)
            combt = jnp.where(rid == gi * N_EXP + e, val, combt)
    combt = jnp.concatenate([combt, jnp.zeros((LANES - N_EXPERTS, tm), F32)], axis=0)
    comb_ref[...] = combt.T


def _outproj(precise, x2d, oa, ob, oc, w_out, g, wrt, brc, tm):
    t = x2d.shape[0]
    row = lambda w: pl.BlockSpec((tm, w), lambda i: (i, 0))
    full = lambda a: pl.BlockSpec(a.shape, lambda i: (0,) * a.ndim)
    return pl.pallas_call(
        functools.partial(_outproj_kernel, precise),
        grid=(t // tm,),
        in_specs=[row(D_MODEL), row(A_WIDTH), row(B_WIDTH), row(C_WIDTH),
                  full(w_out), full(g), full(wrt), full(brc)],
        out_specs=[row(D_MODEL), row(D_MODEL), row(LANES)],
        out_shape=[jax.ShapeDtypeStruct((t, D_MODEL), F32),
                   jax.ShapeDtypeStruct((t, D_MODEL), BF16),
                   jax.ShapeDtypeStruct((t, LANES), F32)],
        compiler_params=pltpu.CompilerParams(
            dimension_semantics=("arbitrary",), vmem_limit_bytes=VMEM_LIMIT),
        name="outproj_router",
    )(x2d, oa, ob, oc, w_out, g, wrt, brc)


def _moe_kernel(final, x2_ref, h2_ref, comb_ref, wg_ref, wu_ref, wd_ref, gf_ref, y_ref, acc_scr):
    e = pl.program_id(1)

    @pl.when(e == 0)
    def _():
        acc_scr[...] = x2_ref[...]

    h = h2_ref[...]
    a = _mm(h, wg_ref[...])
    u = _mm(h, wu_ref[...])
    comb = comb_ref[...]
    lane = lax.broadcasted_iota(jnp.int32, comb.shape, 1)
    cw = jnp.sum(jnp.where(lane == e, comb, 0.0), axis=-1, keepdims=True)
    act = (a * _sigmoid(a)) * u * cw
    acc_scr[...] += _mm(act.astype(BF16), wd_ref[...])

    @pl.when(e == pl.num_programs(1) - 1)
    def _():
        y = acc_scr[...]
        if final:
            y = _rmsnorm_rows(y, gf_ref[...])
        y_ref[...] = y


def _moe(final, x2, h2, comb, wg, wu, wd, gf, tm):
    t = x2.shape[0]
    return pl.pallas_call(
        functools.partial(_moe_kernel, final),
        grid=(t // tm, N_EXPERTS),
        in_specs=[
            pl.BlockSpec((tm, D_MODEL), lambda i, e: (i, 0)),
            pl.BlockSpec((tm, D_MODEL), lambda i, e: (i, 0)),
            pl.BlockSpec((tm, LANES), lambda i, e: (i, 0)),
            pl.BlockSpec((None, D_MODEL, EXPERT_FF), lambda i, e: (e, 0, 0)),
            pl.BlockSpec((None, D_MODEL, EXPERT_FF), lambda i, e: (e, 0, 0)),
            pl.BlockSpec((None, EXPERT_FF, D_MODEL), lambda i, e: (e, 0, 0)),
            pl.BlockSpec(gf.shape, lambda i, e: (0, 0)),
        ],
        out_specs=pl.BlockSpec((tm, D_MODEL), lambda i, e: (i, 0)),
        out_shape=jax.ShapeDtypeStruct((t, D_MODEL), F32),
        scratch_shapes=[pltpu.VMEM((tm, D_MODEL), F32)],
        compiler_params=pltpu.CompilerParams(
            dimension_semantics=("arbitrary", "arbitrary"), vmem_limit_bytes=VMEM_LIMIT),
        name="moe",
    )(x2, h2, comb, wg, wu, wd, gf)


def _block_diag_state(s):
    n = s.shape[0]
    out = jnp.zeros((n, A_HEADS, HEAD_DIM, A_HEADS, HEAD_DIM), s.dtype)
    st = jnp.swapaxes(s, -1, -2)
    for h in range(A_HEADS):
        out = out.at[:, h, :, h, :].set(st[:, h])
    return out.reshape(n, A_WIDTH, A_WIDTH)


def _state_blocks(sbd):
    n = sbd.shape[0]
    s5 = sbd.reshape(n, A_HEADS, HEAD_DIM, A_HEADS, HEAD_DIM)
    return jnp.stack([s5[:, h, :, h, :] for h in range(A_HEADS)], axis=1)


def _pad_rows(a, nb, per, to):
    w = a.shape[-1]
    a3 = a.reshape(nb, per, w)
    return jnp.pad(a3, ((0, 0), (0, to - per), (0, 0))).reshape(nb * to, w)


def kernel(x_prompt, x_sample, cache_k, cache_v, cache_logf, state_hgrn, page_table, norm_mix_gain, w_in, hgrn_lb_logits, hgrn_out_gain, fox_f_bias, fox_out_gain, sgu_v_gain, sgu_w_s, sgu_b, sgu_out_gain, w_out, norm_ffn_gain, router_group_w, router_group_b, router_expert_w, router_expert_b, expert_w_gate, expert_w_up, expert_w_down, norm_final_gain):
    nb, seq, _ = x_prompt.shape
    ndb, dseq, _ = x_sample.shape
    depth = w_in.shape[0]
    tp = nb * seq
    ts = ndb * dseq
    tm_p = 256
    assert seq % ATT_TK == 0 and seq % tm_p == 0
    assert ts % SUBLANES == 0 and ts <= CHUNK and dseq <= SUBLANES

    cm_np, mk_np = _hgrn_consts()
    cm = jnp.asarray(cm_np, BF16)
    mk = jnp.asarray(mk_np, F32)
    e256 = jnp.asarray(_block_ones(A_WIDTH, HEAD_DIM), BF16)
    place_np, qone_np, vone_np = _aug_consts()
    place = jnp.asarray(place_np, BF16)
    qone = jnp.asarray(qone_np)
    vone = jnp.asarray(vone_np)
    ar = np.arange(tm_p)
    tril_tm = jnp.asarray((ar[None, :] <= ar[:, None]).astype(np.float32), BF16)
    ac = np.arange(CHUNK)
    triu_pg = jnp.asarray((ac[:, None] <= ac[None, :]).astype(np.float32), BF16)
    rows = B_HEADS * SUBLANES
    rr = np.arange(rows)
    col = np.arange(B_WIDTH)
    bm = jnp.asarray((rr[:, None] // SUBLANES == col[None, :] // HEAD_DIM).astype(np.float32))
    tq_of_row = rr[:, None] % SUBLANES
    nm = jnp.asarray(((ac[None, :] < dseq) & (ac[None, :] <= tq_of_row)
                      & (tq_of_row < dseq)).astype(np.float32))

    sr = np.arange(ts)
    same_seq = jnp.asarray(((sr[:, None] // dseq) == (sr[None, :] // dseq))
                           & ((sr[None, :] % dseq) <= (sr[:, None] % dseq)))
    tril = jnp.asarray(np.tril(np.ones((CHUNK, CHUNK), np.float32)))

    ck = jnp.transpose(cache_k, (0, 1, 3, 4, 2))
    cv = jnp.transpose(cache_v, (0, 1, 3, 4, 2))
    clf = jnp.transpose(cache_logf, (0, 1, 3, 2))

    xp = x_prompt.reshape(tp, D_MODEL)
    xs = x_sample.reshape(ts, D_MODEL)
    zeros_state = jnp.zeros((nb, A_WIDTH, A_WIDTH), F32)

    outs_p = {k: [] for k in ("k", "v", "lf", "s")}
    outs_s = {k: [] for k in ("k", "v", "lf", "s", "vn")}

    for l in range(depth):
        wl = w_in[l]
        w_r32 = jnp.concatenate(
            [wl[:, 0:2560], wl[:, 2568:3080], wl[:, 2560:2568],
             jnp.zeros((D_MODEL, IN_COLS_PAD - 3080), F32)], axis=1)
        w_r = w_r32.astype(BF16)
        fb = jnp.pad(fox_f_bias[l][None, :], ((0, 0), (0, LANES - B_HEADS)))
        g_mix = norm_mix_gain[l][None, :]
        vg = sgu_v_gain[l][None, :]
        sog = sgu_out_gain[l][None, :]
        smat_p = (sgu_w_s[l] * tril).astype(BF16)
        sbias_p = jnp.repeat(sgu_b[l].T, HEAD_DIM, axis=1)
        w4 = sgu_w_s[l][:, :dseq, :dseq]
        smat_s = jnp.where(same_seq, jnp.tile(w4, (1, ndb, ndb)), 0.0)
        sbias_s = jnp.repeat(jnp.tile(sgu_b[l][:, :dseq].T, (ndb, 1)), HEAD_DIM, axis=1)
        hg = hgrn_out_gain[l][None, :]
        fg = fox_out_gain[l][None, :]
        wo32 = w_out[l]
        wo = wo32.astype(BF16)
        g_ffn = norm_ffn_gain[l][None, :]
        wrt = jnp.concatenate(
            [router_group_w[l].T,
             jnp.transpose(router_expert_w[l], (0, 2, 1)).reshape(N_EXPERTS, D_MODEL),
             jnp.zeros((32 - N_GROUPS - N_EXPERTS, D_MODEL), F32)], axis=0)
        brc = jnp.concatenate(
            [router_group_b[l], router_expert_b[l].reshape(-1),
             jnp.zeros((32 - N_GROUPS - N_EXPERTS,), F32)])[:, None]
        wg = expert_w_gate[l].reshape(N_EXPERTS, D_MODEL, EXPERT_FF).astype(BF16)
        wu = expert_w_up[l].reshape(N_EXPERTS, D_MODEL, EXPERT_FF).astype(BF16)
        wd = expert_w_down[l].reshape(N_EXPERTS, EXPERT_FF, D_MODEL).astype(BF16)
        gf = norm_final_gain[None, :]
        final = l == depth - 1

        consts_p = (g_mix, w_r, hgrn_lb_logits, fb, vg, smat_p, sbias_p, sog, e256,
                    tril_tm, place, qone, vone)
        (qa, lfa, ka, ia, ga, _, kb, vb, lfb, oc, _, qaug, kaug, vaug) = _inproj(
            l, True, seq // tm_p, False, xp, consts_p, tm_p)
        oa, st = _hgrn(False, qa, lfa, ka, ia, ga, zeros_state, cm, mk, e256, hg, nb, seq // CHUNK)
        ob = _fox(qaug, kaug, vaug, fg, nb, seq)
        x2, h2, comb = _outproj(False, xp, oa, ob, oc, wo, g_ffn, wrt, brc, tm_p)
        xp = _moe(final, x2, h2, comb, wg, wu, wd, gf, 1024 if tp % 1024 == 0 else tm_p)
        outs_p["k"].append(kb.reshape(nb, seq, B_HEADS, HEAD_DIM))
        outs_p["v"].append(vb.reshape(nb, seq, B_HEADS, HEAD_DIM))
        outs_p["lf"].append(lfb.reshape(nb, seq, B_HEADS))
        outs_p["s"].append(_state_blocks(st))

        consts_s = (g_mix, w_r32, hgrn_lb_logits, fb, vg, smat_s, sbias_s, sog, e256,
                    tril_tm, place, qone, vone)
        (qa, lfa, ka, ia, ga, qb, kb, vb, lfb, oc, vn, _, _, _) = _inproj(
            l, False, 1, True, xs, consts_s, ts)
        pads = [_pad_rows(a, ndb, dseq, CHUNK) for a in (qa, lfa, ka, ia, ga)]
        oa_pad, st = _hgrn(True, *pads, _block_diag_state(state_hgrn[l]), cm, mk, e256, hg, ndb, 1)
        oa = oa_pad.reshape(ndb, CHUNK, A_WIDTH)[:, :dseq].reshape(ts, A_WIDTH)
        q4 = jnp.pad(qb.reshape(ndb, dseq, B_HEADS, HEAD_DIM),
                     ((0, 0), (0, SUBLANES - dseq), (0, 0), (0, 0)))
        q_bd = (jnp.transpose(q4, (0, 2, 1, 3))[:, :, :, None, :]
                * jnp.eye(B_HEADS, dtype=F32)[None, :, None, :, None]).reshape(ndb, rows, B_WIDTH)
        k5 = kb.reshape(ndb, dseq, B_HEADS, HEAD_DIM)
        v5 = vb.reshape(ndb, dseq, B_HEADS, HEAD_DIM)
        lf3 = lfb.reshape(ndb, dseq, B_HEADS)
        padn = ((0, 0), (0, 0), (0, 0), (0, CHUNK - dseq))
        ob8 = _paged(l, page_table, q_bd, ck, cv, clf,
                     jnp.pad(jnp.transpose(k5, (0, 2, 3, 1)), padn),
                     jnp.pad(jnp.transpose(v5, (0, 2, 3, 1)), padn),
                     jnp.pad(jnp.transpose(lf3, (0, 2, 1)), padn[1:]),
                     triu_pg, bm, nm, fg)
        ob = ob8[:, :dseq].reshape(ts, B_WIDTH)
        x2, h2, comb = _outproj(True, xs, oa, ob, oc, wo32, g_ffn, wrt, brc, ts)
        xs = _moe(final, x2, h2, comb, wg, wu, wd, gf, ts)
        outs_s["k"].append(k5)
        outs_s["v"].append(v5)
        outs_s["lf"].append(lf3)
        outs_s["s"].append(_state_blocks(st))
        outs_s["vn"].append(vn.reshape(ndb, dseq, C_WIDTH))

    return (xp.reshape(nb, seq, D_MODEL), xs.reshape(ndb, dseq, D_MODEL),
            jnp.stack(outs_p["k"]), jnp.stack(outs_p["v"]), jnp.stack(outs_p["lf"]),
            jnp.stack(outs_p["s"]),
            jnp.stack(outs_s["k"]), jnp.stack(outs_s["v"]), jnp.stack(outs_s["lf"]),
            jnp.stack(outs_s["s"]), jnp.stack(outs_s["vn"]))
```

```python
import functools

import numpy as np
import jax
import jax.numpy as jnp
from jax import lax
from jax.experimental import pallas as pl
from jax.experimental.pallas import tpu as pltpu

F32 = jnp.float32
BF16 = jnp.bfloat16

D_MODEL = 1024
HEAD_DIM = 64
A_HEADS = 4
A_WIDTH = A_HEADS * HEAD_DIM
B_HEADS = 8
B_WIDTH = B_HEADS * HEAD_DIM
C_GROUPS = 4
C_WIDTH = C_GROUPS * HEAD_DIM
CHUNK = 128
N_GROUPS = 4
N_EXP = 4
N_EXPERTS = N_GROUPS * N_EXP
EXPERT_FF = D_MODEL // 4
RMS_EPS = 1e-6
NEG_INF = -1e30
LOG2E = 1.4426950408889634
LANES = 128
SUBLANES = 8
IN_COLS_PAD = 3200
AUG_WIDTH = B_HEADS * LANES
HGRN_LEVELS = (1, 2, 4, 8, 16, 32, 64)
ATT_TQ = 512
ATT_TK = 256
PAGES_PER_STEP = 8
VMEM_LIMIT = 48 * 1024 * 1024


class _Mx:
    def __init__(self, precise):
        self.precise = precise
        self.prec = lax.Precision.HIGHEST if precise else None

    def cast(self, x):
        return x.astype(F32) if self.precise else x.astype(BF16)

    def mm(self, a, b):
        return jnp.dot(self.cast(a), self.cast(b), preferred_element_type=F32, precision=self.prec)

    def mm_nt(self, a, b):
        return lax.dot_general(self.cast(a), self.cast(b), (((1,), (1,)), ((), ())),
                               preferred_element_type=F32, precision=self.prec)


def _mm(a, b):
    return jnp.dot(a, b, preferred_element_type=F32)


def _mm_nt(a, b):
    return lax.dot_general(a, b, (((1,), (1,)), ((), ())), preferred_element_type=F32)


def _split3(x):
    p1 = x.astype(BF16)
    r1 = x - p1.astype(F32)
    p2 = r1.astype(BF16)
    r2 = r1 - p2.astype(F32)
    return p1, p2, r2.astype(BF16)


def _split2(x):
    hi = x.astype(BF16)
    return hi, (x - hi.astype(F32)).astype(BF16)


def _dot01_left(m01, x):
    p1, p2, p3 = _split3(x)
    return _mm(m01, p1) + _mm(m01, p2) + _mm(m01, p3)


def _dot01_right(x, m01):
    p1, p2, p3 = _split3(x)
    return _mm(p1, m01) + _mm(p2, m01) + _mm(p3, m01)


def _sigmoid(x):
    return 1.0 / (1.0 + jnp.exp(-x))


def _log_sigmoid(x):
    return jnp.minimum(x, 0.0) - jnp.log1p(jnp.exp(-jnp.abs(x)))


def _gelu_tanh(x):
    return x * (0.5 * (1.0 + jnp.tanh(0.7978845608028654 * (x + 0.044715 * (x * x * x)))))


def _rmsnorm_rows(x, g):
    return x * lax.rsqrt(jnp.mean(x * x, axis=-1, keepdims=True) + RMS_EPS) * g


def _group_rmsnorm(x, e01, g):
    ms = _dot01_right(x * x, e01) * (1.0 / HEAD_DIM)
    return x * lax.rsqrt(ms + RMS_EPS) * g


def _head_blocks(x):
    rows = x.shape[0]
    left = lax.broadcasted_iota(jnp.int32, (rows, LANES), 1) < HEAD_DIM
    out = []
    for pr in range(B_HEADS // 2):
        xp = x[:, pr * LANES:(pr + 1) * LANES]
        out.append(jnp.where(left, xp, 0.0))
        out.append(jnp.where(left, pltpu.roll(xp, HEAD_DIM, 1), 0.0))
    return jnp.concatenate(out, axis=1)


def _block_ones(n, blk):
    i = np.arange(n)
    return (i[:, None] // blk == i[None, :] // blk).astype(np.float32)


def _hgrn_consts():
    n = CHUNK
    t = np.arange(n)[:, None]
    j = np.arange(n)[None, :]
    mats = [j <= t, j > t]
    masks = []
    for h in HGRN_LEVELS:
        same = (t // (2 * h)) == (j // (2 * h))
        off_j = j % (2 * h)
        off_t = t % (2 * h)
        mats.append((same & (off_j >= h) & (j <= t)) | (same & (off_j < h) & (j > t)))
        masks.append(same & (off_t >= h) & (off_j < h))
    cm = np.concatenate([m.astype(np.float32) for m in mats], axis=0)
    mk = np.stack([m.astype(np.float32) for m in masks])
    return cm, mk


def _aug_consts():
    place = np.zeros((3 * LANES, AUG_WIDTH), np.float32)
    qone = np.zeros((1, AUG_WIDTH), np.float32)
    vone = np.zeros((1, AUG_WIDTH), np.float32)
    for h in range(B_HEADS):
        for part in range(3):
            place[part * LANES + h, h * LANES + HEAD_DIM + part] = 1.0
            qone[0, h * LANES + HEAD_DIM + part] = 1.0
        vone[0, h * LANES + HEAD_DIM:(h + 1) * LANES] = 1.0
    return place, qone, vone


def _inproj_kernel(layer, prompt, tiles_per_seq, precise, n_aliased,
                   x_ref, g_ref, w_ref, lbl_ref, fb_ref, vg_ref, smat_ref, sbias_ref, sog_ref,
                   e_ref, tril_ref, place_ref, qone_ref, vone_ref, *rest):
    (qa_ref, lfa_ref, ka_ref, ia_ref, ga_ref, qb_ref, kb_ref, vb_ref, lfb_ref,
     oc_ref, vn_ref, qaug_ref, kaug_ref, vaug_ref, carry_scr) = rest[n_aliased:]
    mx = _Mx(precise)
    tm = x_ref.shape[0]
    h = _rmsnorm_rows(x_ref[...], g_ref[...])
    y = mx.mm(h, w_ref[...])
    a_q = y[:, 0:256]
    z = y[:, 256:512]
    a_i = y[:, 512:768]
    a_g = y[:, 768:1024]
    b_q = y[:, 1024:1536] * (HEAD_DIM ** -0.5)
    b_k = y[:, 1536:2048]
    b_v = y[:, 2048:2560]
    c_u = y[:, 2560:2816]
    c_v = y[:, 2816:3072]
    b_f = y[:, 3072:3200]

    lbl = lbl_ref[...]
    ex = jnp.exp(lbl - jnp.max(lbl, axis=0, keepdims=True))
    p = ex / jnp.sum(ex, axis=0, keepdims=True)
    cs = p[0:1]
    for j in range(1, layer + 1):
        cs = cs + p[j:j + 1]
    lb = cs - p[0:1]
    la = jnp.log(lb)
    lbb = jnp.log1p(-lb) + _log_sigmoid(z)
    lfa_ref[...] = jnp.maximum(la, lbb) + jnp.log1p(jnp.exp(-jnp.abs(la - lbb)))
    ka_ref[...] = (1.0 - lb) * _sigmoid(-z)
    qa_ref[...] = a_q * _sigmoid(a_q)
    ia_ref[...] = a_i
    ga_ref[...] = a_g * _sigmoid(a_g)

    qb_ref[...] = b_q
    kb_ref[...] = b_k
    vb_ref[...] = b_v
    lfb = _log_sigmoid(b_f + fb_ref[...])
    lfb_ref[...] = lfb[:, 0:B_HEADS]

    if prompt:
        i = pl.program_id(0)

        @pl.when(i % tiles_per_seq == 0)
        def _():
            carry_scr[...] = jnp.zeros(carry_scr.shape, F32)

        c = carry_scr[0:1, :] + _dot01_left(tril_ref[...], lfb)
        carry_scr[...] = jnp.broadcast_to(c[tm - 1:tm, :], carry_scr.shape)
        n1, n2, n3 = _split3(c * (-LOG2E))
        bias = _mm(jnp.concatenate([n1, n2, n3], axis=1), place_ref[...])
        qaug_ref[...] = (_head_blocks(b_q * LOG2E) + qone_ref[...]).astype(BF16)
        kaug_ref[...] = (_head_blocks(b_k) + bias).astype(BF16)
        vaug_ref[...] = (_head_blocks(b_v) + vone_ref[...]).astype(BF16)
    else:
        qaug_ref[...] = jnp.zeros(qaug_ref.shape, BF16)
        kaug_ref[...] = jnp.zeros(kaug_ref.shape, BF16)
        vaug_ref[...] = jnp.zeros(vaug_ref.shape, BF16)

    e256 = e_ref[...]
    u = _gelu_tanh(c_u)
    vn = _group_rmsnorm(_gelu_tanh(c_v), e256, vg_ref[...])
    vn_ref[...] = vn
    left = lax.broadcasted_iota(jnp.int32, (CHUNK, LANES), 1) < HEAD_DIM
    zrows = []
    for c in range(tm // CHUNK):
        zp = []
        for pr in range(2):
            vp = vn[c * CHUNK:(c + 1) * CHUNK, pr * LANES:(pr + 1) * LANES]
            zp.append(jnp.where(left, mx.mm(smat_ref[2 * pr], vp), mx.mm(smat_ref[2 * pr + 1], vp)))
        zrows.append(jnp.concatenate(zp, axis=1) + sbias_ref[...])
    zc = zrows[0] if len(zrows) == 1 else jnp.concatenate(zrows, axis=0)
    oc_ref[...] = _group_rmsnorm(u * zc, e256, sog_ref[...]).astype(oc_ref.dtype)


KV_OUT = (6, 7)


def _inproj(layer, prompt, tiles_per_seq, precise, x2d, consts, tm, kv_depth, kv_prev):
    t = x2d.shape[0]
    row = lambda w: pl.BlockSpec((tm, w), lambda i: (i, 0))
    full = lambda a: pl.BlockSpec(a.shape, lambda i: (0,) * a.ndim)
    aug_rows = tm if prompt else SUBLANES
    aug_t = t if prompt else SUBLANES
    aug_spec = pl.BlockSpec((aug_rows, AUG_WIDTH), (lambda i: (i, 0)) if prompt else (lambda i: (0, 0)))
    outs = [(A_WIDTH, F32)] * 5 + [(B_WIDTH, F32)] * 3 + [(B_HEADS, F32), (C_WIDTH, F32 if precise else BF16), (C_WIDTH, F32)]
    out_shape = [jax.ShapeDtypeStruct((t, w), d) for w, d in outs]
    out_specs = [row(w) for w, _ in outs]
    for o in KV_OUT:
        out_shape[o] = jax.ShapeDtypeStruct((kv_depth, t, B_WIDTH), F32)
        out_specs[o] = pl.BlockSpec((None, tm, B_WIDTH), lambda i: (layer % kv_depth, i, 0))
    out_shape += [jax.ShapeDtypeStruct((aug_t, AUG_WIDTH), BF16)] * 3
    out_specs += [aug_spec] * 3
    inputs = [x2d, *consts]
    in_specs = [row(D_MODEL)] + [full(a) for a in consts]
    aliases = {}
    if kv_prev is not None:
        for o, buf in zip(KV_OUT, kv_prev):
            aliases[len(inputs)] = o
            inputs.append(buf)
            in_specs.append(pl.BlockSpec(memory_space=pl.ANY))
    return pl.pallas_call(
        functools.partial(_inproj_kernel, layer, prompt, tiles_per_seq, precise, len(aliases)),
        grid=(t // tm,),
        in_specs=in_specs,
        out_specs=out_specs,
        out_shape=out_shape,
        input_output_aliases=aliases,
        scratch_shapes=[pltpu.VMEM((SUBLANES, LANES), F32)],
        compiler_params=pltpu.CompilerParams(
            dimension_semantics=("arbitrary",), vmem_limit_bytes=VMEM_LIMIT),
        name="inproj",
    )(*inputs)


def _hgrn_kernel(precise, q_ref, lf_ref, k_ref, i_ref, g_ref, st0_ref, cm_ref, mk_ref, e_ref,
                 gain_ref, o_ref, stout_ref, st_scr):
    mx = _Mx(precise)
    c = pl.program_id(1)

    @pl.when(c == 0)
    def _():
        st_scr[...] = st0_ref[...]

    if precise:
        p1, p2, p3 = _split3(lf_ref[...])
        r = _mm(cm_ref[...], jnp.concatenate([p1, p2, p3], axis=1))
        r = r[:, 0:256] + r[:, 256:512] + r[:, 512:768]
    else:
        p1, p2 = _split2(lf_ref[...])
        r = _mm(cm_ref[...], jnp.concatenate([p1, p2], axis=1))
        r = r[:, 0:256] + r[:, 256:512]
    q = q_ref[...]
    k = k_ref[...]
    iv = i_ref[...]
    b = r[0:CHUNK]
    su = r[CHUNK:2 * CHUNK]
    qd = q * jnp.exp(b)
    kd = k * jnp.exp(su)

    left = lax.broadcasted_iota(jnp.int32, (CHUNK, LANES), 1) < HEAD_DIM
    amat = [None] * A_HEADS
    for lv in range(len(HGRN_LEVELS)):
        ex = jnp.exp(r[(lv + 2) * CHUNK:(lv + 3) * CHUNK])
        ql = q * ex
        kl = k * ex
        m = mk_ref[lv]
        for pr in range(2):
            qp = ql[:, pr * LANES:(pr + 1) * LANES]
            kp = kl[:, pr * LANES:(pr + 1) * LANES]
            for hh in range(2):
                qm = jnp.where(left if hh == 0 else jnp.logical_not(left), qp, 0.0)
                s = m * mx.mm_nt(qm, kp)
                hd = 2 * pr + hh
                amat[hd] = s if amat[hd] is None else amat[hd] + s

    e256 = e_ref[...]
    o_pairs = []
    for pr in range(2):
        ip = iv[:, pr * LANES:(pr + 1) * LANES]
        o_pairs.append(jnp.where(left, mx.mm(amat[2 * pr], ip), mx.mm(amat[2 * pr + 1], ip)))
    o_intra = jnp.concatenate(o_pairs, axis=1)
    o_diag = _dot01_right(q * k, e256) * iv
    st = st_scr[...]
    o = mx.mm_nt(qd, st) + o_intra + o_diag
    o_ref[...] = (_group_rmsnorm(o, e256, gain_ref[...]) * g_ref[...]).astype(o_ref.dtype)

    upd = mx.mm(iv.T, kd)
    st_new = st * jnp.exp(b[CHUNK - 1:CHUNK, :]) + e256.astype(F32) * upd
    st_scr[...] = st_new

    @pl.when(c == pl.num_programs(1) - 1)
    def _():
        stout_ref[...] = st_new.T


def _hgrn(precise, qa, lfa, ka, ia, ga, st0, cm, mk, e256, gain, nb, nc):
    t = qa.shape[0]
    row = pl.BlockSpec((CHUNK, A_WIDTH), lambda b, c: (b * nc + c, 0))
    full = lambda a: pl.BlockSpec(a.shape, lambda b, c: (0,) * a.ndim)
    st_spec = pl.BlockSpec((None, A_WIDTH, A_WIDTH), lambda b, c: (b, 0, 0))
    return pl.pallas_call(
        functools.partial(_hgrn_kernel, precise),
        grid=(nb, nc),
        in_specs=[row, row, row, row, row, st_spec, full(cm), full(mk), full(e256), full(gain)],
        out_specs=[row, st_spec],
        out_shape=[jax.ShapeDtypeStruct((t, A_WIDTH), F32 if precise else BF16),
                   jax.ShapeDtypeStruct((nb, A_WIDTH, A_WIDTH), F32)],
        scratch_shapes=[pltpu.VMEM((A_WIDTH, A_WIDTH), F32)],
        compiler_params=pltpu.CompilerParams(
            dimension_semantics=("arbitrary", "arbitrary"), vmem_limit_bytes=VMEM_LIMIT),
        name="hgrn",
    )(qa, lfa, ka, ia, ga, st0, cm, mk, e256, gain)


def _fox_kernel(q_ref, k_ref, v_ref, gain_ref, wms_ref, sh_ref, o_ref, m_scr, acc_scr):
    i = pl.program_id(1)
    tq, tk = ATT_TQ, ATT_TK
    m_scr[...] = jnp.full(m_scr.shape, NEG_INF, F32)
    acc_scr[...] = jnp.zeros(acc_scr.shape, F32)
    n_full = (i * tq) // tk

    def block(r0, nr, ks, mask):
        rows = slice(r0, r0 + nr)
        for hd in range(B_HEADS):
            lanes = slice(hd * LANES, (hd + 1) * LANES)
            s = _mm_nt(q_ref[rows, lanes], k_ref[pl.ds(ks, tk), lanes])
            if mask is not None:
                s = jnp.where(mask, s, NEG_INF)
            m_prev = m_scr[hd, rows, :]
            m_new = jnp.maximum(m_prev, jnp.max(s, axis=-1, keepdims=True))
            alpha = jnp.exp2(m_prev - m_new)
            p = jnp.exp2(s - jnp.concatenate([m_new] * (tk // LANES), axis=1))
            acc_scr[hd, rows, :] = (alpha * acc_scr[hd, rows, :]
                                    + _mm(p.astype(BF16), v_ref[pl.ds(ks, tk), lanes]))
            m_scr[hd, rows, :] = m_new

    def body(j, carry):
        block(0, tq, pl.multiple_of(j * tk, tk), None)
        return carry

    lax.fori_loop(0, n_full, body, 0)
    tri = (lax.broadcasted_iota(jnp.int32, (tk, tk), 1)
           <= lax.broadcasted_iota(jnp.int32, (tk, tk), 0))
    for d in range(tq // tk):
        ks = pl.multiple_of((n_full + d) * tk, tk)
        block(d * tk, tk, ks, tri)
        if (d + 1) * tk < tq:
            block((d + 1) * tk, tq - (d + 1) * tk, ks, None)

    left = lax.broadcasted_iota(jnp.int32, (tq, 2 * LANES), 1) % LANES < HEAD_DIM
    for pr in range(B_HEADS // 2):
        a2 = jnp.concatenate([acc_scr[2 * pr], acc_scr[2 * pr + 1]], axis=1)
        y = a2 * a2
        y = jnp.where(left, y, y * (HEAD_DIM * RMS_EPS))
        yh, yl = _split2(y)
        z = _mm(yh, wms_ref[...]) + _mm(yl, wms_ref[...])
        on = a2 * lax.rsqrt(z * (1.0 / HEAD_DIM)) * gain_ref[:, 2 * pr * LANES:(2 * pr + 2) * LANES]
        onb = on.astype(BF16)
        pair = onb[:, 0:LANES].astype(F32) + _mm(onb[:, LANES:2 * LANES], sh_ref[...])
        o_ref[:, pr * LANES:(pr + 1) * LANES] = pair.astype(BF16)


def _fox(qaug, kaug, vaug, gain, wms, sh, nb, seq):
    t = qaug.shape[0]
    nq = seq // ATT_TQ
    resident = lambda: pl.BlockSpec((seq, AUG_WIDTH), lambda b, i: (b, 0), pipeline_mode=pl.Buffered(1))
    return pl.pallas_call(
        _fox_kernel,
        grid=(nb, nq),
        in_specs=[
            pl.BlockSpec((ATT_TQ, AUG_WIDTH), lambda b, i: (b * nq + i, 0)),
            resident(), resident(),
            pl.BlockSpec(gain.shape, lambda b, i: (0, 0)),
            pl.BlockSpec(wms.shape, lambda b, i: (0, 0)),
            pl.BlockSpec(sh.shape, lambda b, i: (0, 0)),
        ],
        out_specs=pl.BlockSpec((ATT_TQ, B_WIDTH), lambda b, i: (b * nq + i, 0)),
        out_shape=jax.ShapeDtypeStruct((t, B_WIDTH), BF16),
        scratch_shapes=[
            pltpu.VMEM((B_HEADS, ATT_TQ, LANES), F32),
            pltpu.VMEM((B_HEADS, ATT_TQ, LANES), F32),
        ],
        compiler_params=pltpu.CompilerParams(
            dimension_semantics=("arbitrary", "arbitrary"), vmem_limit_bytes=VMEM_LIMIT),
        name="fox_prompt",
    )(qaug, kaug, vaug, gain, wms, sh)


def _paged_kernel(n_in, pt_ref, q_ref, *refs):
    del pt_ref
    g_pages = n_in
    k_refs = refs[0:g_pages]
    v_refs = refs[g_pages:2 * g_pages]
    lf_refs = refs[2 * g_pages:3 * g_pages]
    (kn_ref, vn_ref, lfn_ref, triu_ref, bm_ref, nm_ref, gain_ref, o_ref,
     m_scr, l_scr, acc_scr, carry_scr) = refs[3 * g_pages:]
    p = pl.program_id(1)
    last = pl.num_programs(1) - 1
    rows = q_ref.shape[0]
    flat = B_HEADS * HEAD_DIM

    @pl.when(p == 0)
    def _():
        m_scr[...] = jnp.full(m_scr.shape, NEG_INF, F32)
        l_scr[...] = jnp.zeros(l_scr.shape, F32)
        acc_scr[...] = jnp.zeros(acc_scr.shape, F32)
        carry_scr[...] = jnp.zeros(carry_scr.shape, F32)

    qh, ql = _split2(q_ref[...])
    q2 = jnp.concatenate([qh, ql], axis=0)

    def step(pages, mask):
        carry = carry_scr[...]
        ss = []
        for k_ref, _, lf_ref in pages:
            kh, kl = _split2(k_ref[...].reshape(flat, CHUNK))
            lf = jnp.concatenate([lf_ref[...], jnp.zeros((SUBLANES, CHUNK), F32)], axis=0)
            c = carry + _dot01_right(lf, triu_ref[...])[0:B_HEADS]
            carry = jnp.broadcast_to(c[:, CHUNK - 1:CHUNK], carry.shape)
            sk = _mm(q2, kh)
            s = sk[0:rows] + sk[rows:2 * rows] + _mm(qh, kl)
            s = (s.reshape(B_HEADS, SUBLANES, CHUNK) - c[:, None, :]).reshape(rows, CHUNK)
            if mask is not None:
                s = jnp.where(mask > 0.0, s, NEG_INF)
            ss.append(s)
        carry_scr[...] = carry
        s_all = ss[0] if len(ss) == 1 else jnp.concatenate(ss, axis=1)
        m_prev = m_scr[...]
        m_new = jnp.maximum(m_prev, jnp.max(s_all, axis=-1, keepdims=True))
        alpha = jnp.exp(m_prev - m_new)
        p_all = jnp.exp(s_all - jnp.concatenate([m_new] * len(ss), axis=1))
        l_scr[...] = alpha * l_scr[...] + jnp.sum(p_all, axis=-1, keepdims=True)
        m_scr[...] = m_new
        pv = None
        for g, (_, v_ref, _) in enumerate(pages):
            vh, vl = _split2(v_ref[...].reshape(flat, CHUNK))
            ph, plo = _split2(p_all[:, g * CHUNK:(g + 1) * CHUNK])
            pk = _mm_nt(jnp.concatenate([ph, plo], axis=0), vh)
            d = pk[0:rows] + pk[rows:2 * rows] + _mm_nt(ph, vl)
            pv = d if pv is None else pv + d
        acc_scr[...] = jnp.concatenate([alpha] * (flat // LANES), axis=1) * acc_scr[...] + pv

    @pl.when(p < last)
    def _():
        step(list(zip(k_refs, v_refs, lf_refs)), None)

    @pl.when(p == last)
    def _():
        step([(kn_ref, vn_ref, lfn_ref)], nm_ref[...])
        o = acc_scr[...] / jnp.concatenate([l_scr[...]] * (flat // LANES), axis=1)
        o = o * bm_ref[...]
        ms = jnp.sum(o * o, axis=-1, keepdims=True) * (1.0 / HEAD_DIM)
        on = o * lax.rsqrt(ms + RMS_EPS) * gain_ref[...]
        o_ref[...] = jnp.sum(on.reshape(B_HEADS, SUBLANES, flat), axis=0)


def _paged(layer, page_table, q_bd, ck, cv, clf, kn, vn, lfn, triu, bm, nm, gain):
    nb, n_pages = page_table.shape
    rows = q_bd.shape[1]
    g_pages = min(PAGES_PER_STEP, n_pages)
    assert n_pages % g_pages == 0
    n_steps = n_pages // g_pages

    def page_spec(g, minor):
        def idx(b, p, pt):
            return (layer, pt[b, jnp.minimum(p, n_steps - 1) * g_pages + g]) + (0,) * len(minor)
        return pl.BlockSpec((None, None) + minor, idx)

    kv_minor = (B_HEADS, HEAD_DIM, CHUNK)
    full2 = lambda a: pl.BlockSpec(a.shape, lambda b, p, pt: (0, 0))
    in_specs = [pl.BlockSpec((None, rows, B_WIDTH), lambda b, p, pt: (b, 0, 0))]
    in_specs += [page_spec(g, kv_minor) for g in range(g_pages)]
    in_specs += [page_spec(g, kv_minor) for g in range(g_pages)]
    in_specs += [page_spec(g, (B_HEADS, CHUNK)) for g in range(g_pages)]
    in_specs += [
        pl.BlockSpec((None,) + kv_minor, lambda b, p, pt: (b, 0, 0, 0)),
        pl.BlockSpec((None,) + kv_minor, lambda b, p, pt: (b, 0, 0, 0)),
        pl.BlockSpec((None, B_HEADS, CHUNK), lambda b, p, pt: (b, 0, 0)),
        full2(triu), full2(bm), full2(nm), full2(gain),
    ]
    grid_spec = pltpu.PrefetchScalarGridSpec(
        num_scalar_prefetch=1,
        grid=(nb, n_steps + 1),
        in_specs=in_specs,
        out_specs=pl.BlockSpec((None, SUBLANES, B_WIDTH), lambda b, p, pt: (b, 0, 0)),
        scratch_shapes=[
            pltpu.VMEM((rows, LANES), F32),
            pltpu.VMEM((rows, LANES), F32),
            pltpu.VMEM((rows, B_WIDTH), F32),
            pltpu.VMEM((B_HEADS, CHUNK), F32),
        ],
    )
    return pl.pallas_call(
        functools.partial(_paged_kernel, g_pages),
        grid_spec=grid_spec,
        out_shape=jax.ShapeDtypeStruct((nb, SUBLANES, B_WIDTH), F32),
        compiler_params=pltpu.CompilerParams(
            dimension_semantics=("arbitrary", "arbitrary"), vmem_limit_bytes=VMEM_LIMIT),
        name="fox_paged",
    )(page_table, q_bd, *([ck] * g_pages), *([cv] * g_pages), *([clf] * g_pages),
      kn, vn, lfn, triu, bm, nm, gain)


def _outproj_kernel(precise, x_ref, oa_ref, ob_ref, oc_ref, w_ref, g_ref, wrt_ref, brc_ref,
                    x2_ref, h2_ref, comb_ref):
    mx = _Mx(precise)
    tm = x_ref.shape[0]
    mix = (mx.mm(oa_ref[...], w_ref[0:A_WIDTH, :])
           + mx.mm(ob_ref[...], w_ref[A_WIDTH:A_WIDTH + B_WIDTH, :])
           + mx.mm(oc_ref[...], w_ref[A_WIDTH + B_WIDTH:, :]))
    x2 = x_ref[...] + mix
    x2_ref[...] = x2
    h2 = _rmsnorm_rows(x2, g_ref[...])
    h2b = h2.astype(BF16)
    h2_ref[...] = h2b
    h2l = (h2 - h2b.astype(F32)).astype(BF16)
    wr = wrt_ref[...]
    wrh = wr.astype(BF16)
    wrl = (wr - wrh.astype(F32)).astype(BF16)
    lg = _mm_nt(wrh, h2b) + _mm_nt(wrh, h2l) + _mm_nt(wrl, h2b) + brc_ref[...]
    row = [lg[i:i + 1, :] for i in range(N_GROUPS + N_EXPERTS)]

    g = row[0:N_GROUPS]
    gmax = jnp.maximum(jnp.maximum(g[0], g[1]), jnp.maximum(g[2], g[3]))
    gsel = jnp.where(g[0] == gmax, 0, jnp.where(g[1] == gmax, 1, jnp.where(g[2] == gmax, 2, 3)))
    den = (jnp.exp(g[0] - gmax) + jnp.exp(g[1] - gmax)) + (jnp.exp(g[2] - gmax) + jnp.exp(g[3] - gmax))
    gate = 1.0 / den
    le = []
    for e in range(N_EXP):
        le.append(jnp.where(gsel == 0, row[4 + e],
                            jnp.where(gsel == 1, row[8 + e],
                                      jnp.where(gsel == 2, row[12 + e], row[16 + e]))))
    v1 = jnp.maximum(jnp.maximum(le[0], le[1]), jnp.maximum(le[2], le[3]))
    i1 = jnp.where(le[0] == v1, 0, jnp.where(le[1] == v1, 1, jnp.where(le[2] == v1, 2, 3)))
    le2 = [jnp.where(i1 == e, -jnp.inf, le[e]) for e in range(N_EXP)]
    v2 = jnp.maximum(jnp.maximum(le2[0], le2[1]), jnp.maximum(le2[2], le2[3]))
    i2 = jnp.where(le2[0] == v2, 0, jnp.where(le2[1] == v2, 1, jnp.where(le2[2] == v2, 2, 3)))
    ex = jnp.exp(v2 - v1)
    w1 = 1.0 / (1.0 + ex)
    w2 = ex * w1
    rid = lax.broadcasted_iota(jnp.int32, (N_EXPERTS, tm), 0)
    combt = jnp.zeros((N_EXPERTS, tm), F32)
    for gi in range(N_GROUPS):
        for e in range(N_EXP):
            fine = jnp.where(i1 == e, w1, 0.0) + jnp.where(i2 == e, w2, 0.0)
            val = jnp.where(gsel == gi, gate * fine, 0.0)
            combt = jnp.where(rid == gi * N_EXP + e, val, combt)
    combt = jnp.concatenate([combt, jnp.zeros((LANES - N_EXPERTS, tm), F32)], axis=0)
    comb_ref[...] = combt.T


def _outproj(precise, x2d, oa, ob, oc, w_out, g, wrt, brc, tm):
    t = x2d.shape[0]
    row = lambda w: pl.BlockSpec((tm, w), lambda i: (i, 0))
    full = lambda a: pl.BlockSpec(a.shape, lambda i: (0,) * a.ndim)
    return pl.pallas_call(
        functools.partial(_outproj_kernel, precise),
        grid=(t // tm,),
        in_specs=[row(D_MODEL), row(A_WIDTH), row(B_WIDTH), row(C_WIDTH),
                  full(w_out), full(g), full(wrt), full(brc)],
        out_specs=[row(D_MODEL), row(D_MODEL), row(LANES)],
        out_shape=[jax.ShapeDtypeStruct((t, D_MODEL), F32),
                   jax.ShapeDtypeStruct((t, D_MODEL), BF16),
                   jax.ShapeDtypeStruct((t, LANES), F32)],
        compiler_params=pltpu.CompilerParams(
            dimension_semantics=("arbitrary",), vmem_limit_bytes=VMEM_LIMIT),
        name="outproj_router",
    )(x2d, oa, ob, oc, w_out, g, wrt, brc)


def _moe_kernel(final, x2_ref, h2_ref, comb_ref, wg_ref, wu_ref, wd_ref, gf_ref, y_ref, acc_scr):
    e = pl.program_id(1)

    @pl.when(e == 0)
    def _():
        acc_scr[...] = x2_ref[...]

    h = h2_ref[...]
    a = _mm(h, wg_ref[...])
    u = _mm(h, wu_ref[...])
    comb = comb_ref[...]
    lane = lax.broadcasted_iota(jnp.int32, comb.shape, 1)
    cw = jnp.sum(jnp.where(lane == e, comb, 0.0), axis=-1, keepdims=True)
    act = (a * _sigmoid(a)) * u * cw
    acc_scr[...] += _mm(act.astype(BF16), wd_ref[...])

    @pl.when(e == pl.num_programs(1) - 1)
    def _():
        y = acc_scr[...]
        if final:
            y = _rmsnorm_rows(y, gf_ref[...])
        y_ref[...] = y


def _moe(final, x2, h2, comb, wg, wu, wd, gf, tm):
    t = x2.shape[0]
    return pl.pallas_call(
        functools.partial(_moe_kernel, final),
        grid=(t // tm, N_EXPERTS),
        in_specs=[
            pl.BlockSpec((tm, D_MODEL), lambda i, e: (i, 0)),
            pl.BlockSpec((tm, D_MODEL), lambda i, e: (i, 0)),
            pl.BlockSpec((tm, LANES), lambda i, e: (i, 0)),
            pl.BlockSpec((None, D_MODEL, EXPERT_FF), lambda i, e: (e, 0, 0)),
            pl.BlockSpec((None, D_MODEL, EXPERT_FF), lambda i, e: (e, 0, 0)),
            pl.BlockSpec((None, EXPERT_FF, D_MODEL), lambda i, e: (e, 0, 0)),
            pl.BlockSpec(gf.shape, lambda i, e: (0, 0)),
        ],
        out_specs=pl.BlockSpec((tm, D_MODEL), lambda i, e: (i, 0)),
        out_shape=jax.ShapeDtypeStruct((t, D_MODEL), F32),
        scratch_shapes=[pltpu.VMEM((tm, D_MODEL), F32)],
        compiler_params=pltpu.CompilerParams(
            dimension_semantics=("arbitrary", "arbitrary"), vmem_limit_bytes=VMEM_LIMIT),
        name="moe",
    )(x2, h2, comb, wg, wu, wd, gf)


def _block_diag_state(s):
    n = s.shape[0]
    out = jnp.zeros((n, A_HEADS, HEAD_DIM, A_HEADS, HEAD_DIM), s.dtype)
    st = jnp.swapaxes(s, -1, -2)
    for h in range(A_HEADS):
        out = out.at[:, h, :, h, :].set(st[:, h])
    return out.reshape(n, A_WIDTH, A_WIDTH)


def _state_blocks(sbd):
    n = sbd.shape[0]
    s5 = sbd.reshape(n, A_HEADS, HEAD_DIM, A_HEADS, HEAD_DIM)
    return jnp.stack([s5[:, h, :, h, :] for h in range(A_HEADS)], axis=1)


def _pad_rows(a, nb, per, to):
    w = a.shape[-1]
    a3 = a.reshape(nb, per, w)
    return jnp.pad(a3, ((0, 0), (0, to - per), (0, 0))).reshape(nb * to, w)


def kernel(x_prompt, x_sample, cache_k, cache_v, cache_logf, state_hgrn, page_table, norm_mix_gain, w_in, hgrn_lb_logits, hgrn_out_gain, fox_f_bias, fox_out_gain, sgu_v_gain, sgu_w_s, sgu_b, sgu_out_gain, w_out, norm_ffn_gain, router_group_w, router_group_b, router_expert_w, router_expert_b, expert_w_gate, expert_w_up, expert_w_down, norm_final_gain):
    nb, seq, _ = x_prompt.shape
    ndb, dseq, _ = x_sample.shape
    depth = w_in.shape[0]
    tp = nb * seq
    ts = ndb * dseq
    tm_p = 256
    assert seq % ATT_TQ == 0 and ATT_TQ % ATT_TK == 0 and seq % tm_p == 0
    assert ts % SUBLANES == 0 and ts <= CHUNK and dseq <= SUBLANES

    cm_np, mk_np = _hgrn_consts()
    cm = jnp.asarray(cm_np, BF16)
    mk = jnp.asarray(mk_np, F32)
    e256 = jnp.asarray(_block_ones(A_WIDTH, HEAD_DIM), BF16)
    place_np, qone_np, vone_np = _aug_consts()
    place = jnp.asarray(place_np, BF16)
    wms_np = np.zeros((2 * LANES, 2 * LANES), np.float32)
    sh_np = np.zeros((LANES, LANES), np.float32)
    for blk in range(2):
        wms_np[blk * LANES:blk * LANES + HEAD_DIM + 1, blk * LANES:(blk + 1) * LANES] = 1.0
    sh_np[np.arange(HEAD_DIM), HEAD_DIM + np.arange(HEAD_DIM)] = 1.0
    wms = jnp.asarray(wms_np, BF16)
    sh = jnp.asarray(sh_np, BF16)
    qone = jnp.asarray(qone_np)
    vone = jnp.asarray(vone_np)
    ar = np.arange(tm_p)
    tril_tm = jnp.asarray((ar[None, :] <= ar[:, None]).astype(np.float32), BF16)
    ac = np.arange(CHUNK)
    triu_pg = jnp.asarray((ac[:, None] <= ac[None, :]).astype(np.float32), BF16)
    rows = B_HEADS * SUBLANES
    rr = np.arange(rows)
    col = np.arange(B_WIDTH)
    bm = jnp.asarray((rr[:, None] // SUBLANES == col[None, :] // HEAD_DIM).astype(np.float32))
    tq_of_row = rr[:, None] % SUBLANES
    nm = jnp.asarray(((ac[None, :] < dseq) & (ac[None, :] <= tq_of_row)
                      & (tq_of_row < dseq)).astype(np.float32))

    sr = np.arange(ts)
    same_seq = jnp.asarray(((sr[:, None] // dseq) == (sr[None, :] // dseq))
                           & ((sr[None, :] % dseq) <= (sr[:, None] % dseq)))
    tril = jnp.asarray(np.tril(np.ones((CHUNK, CHUNK), np.float32)))

    ck = jnp.transpose(cache_k, (0, 1, 3, 4, 2))
    cv = jnp.transpose(cache_v, (0, 1, 3, 4, 2))
    clf = jnp.transpose(cache_logf, (0, 1, 3, 2))

    xp = x_prompt.reshape(tp, D_MODEL)
    xs = x_sample.reshape(ts, D_MODEL)
    zeros_state = jnp.zeros((nb, A_WIDTH, A_WIDTH), F32)

    outs_p = {k: [] for k in ("lf", "s")}
    kv_p = None
    outs_s = {k: [] for k in ("k", "v", "lf", "s", "vn")}

    for l in range(depth):
        wl = w_in[l]
        w_r32 = jnp.concatenate(
            [wl[:, 0:2560], wl[:, 2568:3080], wl[:, 2560:2568],
             jnp.zeros((D_MODEL, IN_COLS_PAD - 3080), F32)], axis=1)
        w_r = w_r32.astype(BF16)
        fb = jnp.pad(fox_f_bias[l][None, :], ((0, 0), (0, LANES - B_HEADS)))
        g_mix = norm_mix_gain[l][None, :]
        vg = sgu_v_gain[l][None, :]
        sog = sgu_out_gain[l][None, :]
        smat_p = (sgu_w_s[l] * tril).astype(BF16)
        sbias_p = jnp.repeat(sgu_b[l].T, HEAD_DIM, axis=1)
        w4 = sgu_w_s[l][:, :dseq, :dseq]
        smat_s = jnp.where(same_seq, jnp.tile(w4, (1, ndb, ndb)), 0.0)
        sbias_s = jnp.repeat(jnp.tile(sgu_b[l][:, :dseq].T, (ndb, 1)), HEAD_DIM, axis=1)
        hg = hgrn_out_gain[l][None, :]
        fg = fox_out_gain[l][None, :]
        fg_blk = jnp.pad(fox_out_gain[l].reshape(B_HEADS, HEAD_DIM),
                         ((0, 0), (0, LANES - HEAD_DIM))).reshape(1, AUG_WIDTH)
        wo32 = w_out[l]
        wo = wo32.astype(BF16)
        g_ffn = norm_ffn_gain[l][None, :]
        wrt = jnp.concatenate(
            [router_group_w[l].T,
             jnp.transpose(router_expert_w[l], (0, 2, 1)).reshape(N_EXPERTS, D_MODEL),
             jnp.zeros((32 - N_GROUPS - N_EXPERTS, D_MODEL), F32)], axis=0)
        brc = jnp.concatenate(
            [router_group_b[l], router_expert_b[l].reshape(-1),
             jnp.zeros((32 - N_GROUPS - N_EXPERTS,), F32)])[:, None]
        wg = expert_w_gate[l].reshape(N_EXPERTS, D_MODEL, EXPERT_FF).astype(BF16)
        wu = expert_w_up[l].reshape(N_EXPERTS, D_MODEL, EXPERT_FF).astype(BF16)
        wd = expert_w_down[l].reshape(N_EXPERTS, EXPERT_FF, D_MODEL).astype(BF16)
        gf = norm_final_gain[None, :]
        final = l == depth - 1

        consts_p = (g_mix, w_r, hgrn_lb_logits, fb, vg, smat_p, sbias_p, sog, e256,
                    tril_tm, place, qone, vone)
        (qa, lfa, ka, ia, ga, _, kb, vb, lfb, oc, _, qaug, kaug, vaug) = _inproj(
            l, True, seq // tm_p, False, xp, consts_p, tm_p, depth, kv_p)
        kv_p = (kb, vb)
        oa, st = _hgrn(False, qa, lfa, ka, ia, ga, zeros_state, cm, mk, e256, hg, nb, seq // CHUNK)
        ob = _fox(qaug, kaug, vaug, fg_blk, wms, sh, nb, seq)
        x2, h2, comb = _outproj(False, xp, oa, ob, oc, wo, g_ffn, wrt, brc, tm_p)
        xp = _moe(final, x2, h2, comb, wg, wu, wd, gf, 1024 if tp % 1024 == 0 else tm_p)
        outs_p["lf"].append(lfb.reshape(nb, seq, B_HEADS))
        outs_p["s"].append(_state_blocks(st))

        consts_s = (g_mix, w_r32, hgrn_lb_logits, fb, vg, smat_s, sbias_s, sog, e256,
                    tril_tm, place, qone, vone)
        (qa, lfa, ka, ia, ga, qb, kb, vb, lfb, oc, vn, _, _, _) = _inproj(
            l, False, 1, True, xs, consts_s, ts, 1, None)
        pads = [_pad_rows(a, ndb, dseq, CHUNK) for a in (qa, lfa, ka, ia, ga)]
        oa_pad, st = _hgrn(True, *pads, _block_diag_state(state_hgrn[l]), cm, mk, e256, hg, ndb, 1)
        oa = oa_pad.reshape(ndb, CHUNK, A_WIDTH)[:, :dseq].reshape(ts, A_WIDTH)
        q4 = jnp.pad(qb.reshape(ndb, dseq, B_HEADS, HEAD_DIM),
                     ((0, 0), (0, SUBLANES - dseq), (0, 0), (0, 0)))
        q_bd = (jnp.transpose(q4, (0, 2, 1, 3))[:, :, :, None, :]
                * jnp.eye(B_HEADS, dtype=F32)[None, :, None, :, None]).reshape(ndb, rows, B_WIDTH)
        k5 = kb.reshape(ndb, dseq, B_HEADS, HEAD_DIM)
        v5 = vb.reshape(ndb, dseq, B_HEADS, HEAD_DIM)
        lf3 = lfb.reshape(ndb, dseq, B_HEADS)
        padn = ((0, 0), (0, 0), (0, 0), (0, CHUNK - dseq))
        ob8 = _paged(l, page_table, q_bd, ck, cv, clf,
                     jnp.pad(jnp.transpose(k5, (0, 2, 3, 1)), padn),
                     jnp.pad(jnp.transpose(v5, (0, 2, 3, 1)), padn),
                     jnp.pad(jnp.transpose(lf3, (0, 2, 1)), padn[1:]),
                     triu_pg, bm, nm, fg)
        ob = ob8[:, :dseq].reshape(ts, B_WIDTH)
        x2, h2, comb = _outproj(True, xs, oa, ob, oc, wo32, g_ffn, wrt, brc, ts)
        xs = _moe(final, x2, h2, comb, wg, wu, wd, gf, ts)
        outs_s["k"].append(k5)
        outs_s["v"].append(v5)
        outs_s["lf"].append(lf3)
        outs_s["s"].append(_state_blocks(st))
        outs_s["vn"].append(vn.reshape(ndb, dseq, C_WIDTH))

    return (xp.reshape(nb, seq, D_MODEL), xs.reshape(ndb, dseq, D_MODEL),
            kv_p[0].reshape(depth, nb, seq, B_HEADS, HEAD_DIM),
            kv_p[1].reshape(depth, nb, seq, B_HEADS, HEAD_DIM), jnp.stack(outs_p["lf"]),
            jnp.stack(outs_p["s"]),
            jnp.stack(outs_s["k"]), jnp.stack(outs_s["v"]), jnp.stack(outs_s["lf"]),
            jnp.stack(outs_s["s"]), jnp.stack(outs_s["vn"]))
```

```python
import functools

import numpy as np
import jax
import jax.numpy as jnp
from jax import lax
from jax.experimental import pallas as pl
from jax.experimental.pallas import tpu as pltpu

F32 = jnp.float32
BF16 = jnp.bfloat16

D_MODEL = 1024
HEAD_DIM = 64
A_HEADS = 4
A_WIDTH = A_HEADS * HEAD_DIM
B_HEADS = 8
B_WIDTH = B_HEADS * HEAD_DIM
C_GROUPS = 4
C_WIDTH = C_GROUPS * HEAD_DIM
CHUNK = 128
N_GROUPS = 4
N_EXP = 4
N_EXPERTS = N_GROUPS * N_EXP
EXPERT_FF = D_MODEL // 4
RMS_EPS = 1e-6
NEG_INF = -1e30
LOG2E = 1.4426950408889634
LANES = 128
SUBLANES = 8
IN_COLS_PAD = 3200
AUG_WIDTH = B_HEADS * LANES
HGRN_LEVELS = (1, 2, 4, 8, 16, 32, 64)
ATT_TQ = 512
ATT_TK = 256
PAGES_PER_STEP = 8
VMEM_LIMIT = 48 * 1024 * 1024


class _Mx:
    def __init__(self, precise):
        self.precise = precise
        self.prec = lax.Precision.HIGHEST if precise else None

    def cast(self, x):
        return x.astype(F32) if self.precise else x.astype(BF16)

    def mm(self, a, b):
        return jnp.dot(self.cast(a), self.cast(b), preferred_element_type=F32, precision=self.prec)

    def mm_nt(self, a, b):
        return lax.dot_general(self.cast(a), self.cast(b), (((1,), (1,)), ((), ())),
                               preferred_element_type=F32, precision=self.prec)


def _mm(a, b):
    return jnp.dot(a, b, preferred_element_type=F32)


def _mm_nt(a, b):
    return lax.dot_general(a, b, (((1,), (1,)), ((), ())), preferred_element_type=F32)


def _split3(x):
    p1 = x.astype(BF16)
    r1 = x - p1.astype(F32)
    p2 = r1.astype(BF16)
    r2 = r1 - p2.astype(F32)
    return p1, p2, r2.astype(BF16)


def _split2(x):
    hi = x.astype(BF16)
    return hi, (x - hi.astype(F32)).astype(BF16)


def _dot01_left(m01, x):
    p1, p2, p3 = _split3(x)
    return _mm(m01, p1) + _mm(m01, p2) + _mm(m01, p3)


def _dot01_right(x, m01):
    p1, p2, p3 = _split3(x)
    return _mm(p1, m01) + _mm(p2, m01) + _mm(p3, m01)


def _sigmoid(x):
    return 1.0 / (1.0 + jnp.exp(-x))


def _log_sigmoid(x):
    return jnp.minimum(x, 0.0) - jnp.log1p(jnp.exp(-jnp.abs(x)))


def _gelu_tanh(x):
    return x * (0.5 * (1.0 + jnp.tanh(0.7978845608028654 * (x + 0.044715 * (x * x * x)))))


def _rmsnorm_rows(x, g):
    return x * lax.rsqrt(jnp.mean(x * x, axis=-1, keepdims=True) + RMS_EPS) * g


def _group_rmsnorm(x, e01, g):
    ms = _dot01_right(x * x, e01) * (1.0 / HEAD_DIM)
    return x * lax.rsqrt(ms + RMS_EPS) * g


def _head_blocks(x):
    rows = x.shape[0]
    left = lax.broadcasted_iota(jnp.int32, (rows, LANES), 1) < HEAD_DIM
    out = []
    for pr in range(B_HEADS // 2):
        xp = x[:, pr * LANES:(pr + 1) * LANES]
        out.append(jnp.where(left, xp, 0.0))
        out.append(jnp.where(left, pltpu.roll(xp, HEAD_DIM, 1), 0.0))
    return jnp.concatenate(out, axis=1)


def _block_ones(n, blk):
    i = np.arange(n)
    return (i[:, None] // blk == i[None, :] // blk).astype(np.float32)


def _hgrn_consts():
    n = CHUNK
    t = np.arange(n)[:, None]
    j = np.arange(n)[None, :]
    mats = [j <= t, j > t]
    masks = []
    for h in HGRN_LEVELS:
        same = (t // (2 * h)) == (j // (2 * h))
        off_j = j % (2 * h)
        off_t = t % (2 * h)
        mats.append((same & (off_j >= h) & (j <= t)) | (same & (off_j < h) & (j > t)))
        masks.append(same & (off_t >= h) & (off_j < h))
    cm = np.concatenate([m.astype(np.float32) for m in mats], axis=0)
    mk = np.stack([m.astype(np.float32) for m in masks])
    return cm, mk


def _aug_consts():
    place = np.zeros((3 * LANES, AUG_WIDTH), np.float32)
    qone = np.zeros((1, AUG_WIDTH), np.float32)
    vone = np.zeros((1, AUG_WIDTH), np.float32)
    for h in range(B_HEADS):
        for part in range(3):
            place[part * LANES + h, h * LANES + HEAD_DIM + part] = 1.0
            qone[0, h * LANES + HEAD_DIM + part] = 1.0
        vone[0, h * LANES + HEAD_DIM:(h + 1) * LANES] = 1.0
    return place, qone, vone


def _inproj_kernel(layer, prompt, tiles_per_seq, precise, n_aliased,
                   x_ref, g_ref, w_ref, lbl_ref, fb_ref, vg_ref, smat_ref, sbias_ref, sog_ref,
                   e_ref, tril_ref, place_ref, qone_ref, vone_ref, *rest):
    (qa_ref, lfa_ref, ka_ref, ia_ref, ga_ref, qb_ref, kb_ref, vb_ref, lfb_ref,
     oc_ref, vn_ref, qaug_ref, kaug_ref, vaug_ref, carry_scr) = rest[n_aliased:]
    mx = _Mx(precise)
    tm = x_ref.shape[0]
    h = _rmsnorm_rows(x_ref[...], g_ref[...])
    y = mx.mm(h, w_ref[...])
    a_q = y[:, 0:256]
    z = y[:, 256:512]
    a_i = y[:, 512:768]
    a_g = y[:, 768:1024]
    b_q = y[:, 1024:1536] * (HEAD_DIM ** -0.5)
    b_k = y[:, 1536:2048]
    b_v = y[:, 2048:2560]
    c_u = y[:, 2560:2816]
    c_v = y[:, 2816:3072]
    b_f = y[:, 3072:3200]

    lbl = lbl_ref[...]
    ex = jnp.exp(lbl - jnp.max(lbl, axis=0, keepdims=True))
    p = ex / jnp.sum(ex, axis=0, keepdims=True)
    cs = p[0:1]
    for j in range(1, layer + 1):
        cs = cs + p[j:j + 1]
    lb = cs - p[0:1]
    la = jnp.log(lb)
    lbb = jnp.log1p(-lb) + _log_sigmoid(z)
    lfa_ref[...] = jnp.maximum(la, lbb) + jnp.log1p(jnp.exp(-jnp.abs(la - lbb)))
    ka_ref[...] = (1.0 - lb) * _sigmoid(-z)
    qa_ref[...] = a_q * _sigmoid(a_q)
    ia_ref[...] = a_i
    ga_ref[...] = a_g * _sigmoid(a_g)

    qb_ref[...] = b_q
    if prompt:
        kb_ref[...] = b_k.T
        vb_ref[...] = b_v.T
    else:
        kb_ref[...] = b_k
        vb_ref[...] = b_v
    lfb = _log_sigmoid(b_f + fb_ref[...])
    lfb_ref[...] = lfb[:, 0:B_HEADS]

    if prompt:
        i = pl.program_id(0)

        @pl.when(i % tiles_per_seq == 0)
        def _():
            carry_scr[...] = jnp.zeros(carry_scr.shape, F32)

        c = carry_scr[0:1, :] + _dot01_left(tril_ref[...], lfb)
        carry_scr[...] = jnp.broadcast_to(c[tm - 1:tm, :], carry_scr.shape)
        n1, n2, n3 = _split3(c * (-LOG2E))
        bias = _mm(jnp.concatenate([n1, n2, n3], axis=1), place_ref[...])
        qaug_ref[...] = (_head_blocks(b_q * LOG2E) + qone_ref[...]).astype(BF16)
        kaug_ref[...] = (_head_blocks(b_k) + bias).astype(BF16)
        vaug_ref[...] = (_head_blocks(b_v) + vone_ref[...]).astype(BF16)
    else:
        qaug_ref[...] = jnp.zeros(qaug_ref.shape, BF16)
        kaug_ref[...] = jnp.zeros(kaug_ref.shape, BF16)
        vaug_ref[...] = jnp.zeros(vaug_ref.shape, BF16)

    e256 = e_ref[...]
    u = _gelu_tanh(c_u)
    vn = _group_rmsnorm(_gelu_tanh(c_v), e256, vg_ref[...])
    vn_ref[...] = vn
    left = lax.broadcasted_iota(jnp.int32, (CHUNK, LANES), 1) < HEAD_DIM
    zrows = []
    for c in range(tm // CHUNK):
        zp = []
        for pr in range(2):
            vp = vn[c * CHUNK:(c + 1) * CHUNK, pr * LANES:(pr + 1) * LANES]
            zp.append(jnp.where(left, mx.mm(smat_ref[2 * pr], vp), mx.mm(smat_ref[2 * pr + 1], vp)))
        zrows.append(jnp.concatenate(zp, axis=1) + sbias_ref[...])
    zc = zrows[0] if len(zrows) == 1 else jnp.concatenate(zrows, axis=0)
    oc_ref[...] = _group_rmsnorm(u * zc, e256, sog_ref[...]).astype(oc_ref.dtype)


KV_OUT = (6, 7)


def _inproj(layer, prompt, tiles_per_seq, precise, x2d, consts, tm, kv_depth, kv_prev):
    t = x2d.shape[0]
    row = lambda w: pl.BlockSpec((tm, w), lambda i: (i, 0))
    full = lambda a: pl.BlockSpec(a.shape, lambda i: (0,) * a.ndim)
    aug_rows = tm if prompt else SUBLANES
    aug_t = t if prompt else SUBLANES
    aug_spec = pl.BlockSpec((aug_rows, AUG_WIDTH), (lambda i: (i, 0)) if prompt else (lambda i: (0, 0)))
    outs = [(A_WIDTH, F32)] * 5 + [(B_WIDTH, F32)] * 3 + [(B_HEADS, F32), (C_WIDTH, F32 if precise else BF16), (C_WIDTH, F32)]
    out_shape = [jax.ShapeDtypeStruct((t, w), d) for w, d in outs]
    out_specs = [row(w) for w, _ in outs]
    for o in KV_OUT:
        if prompt:
            seq = tiles_per_seq * tm
            out_shape[o] = jax.ShapeDtypeStruct((kv_depth, t // seq, B_WIDTH, seq), F32)
            out_specs[o] = pl.BlockSpec(
                (None, None, B_WIDTH, tm),
                lambda i: (layer, i // tiles_per_seq, 0, i % tiles_per_seq))
        else:
            out_shape[o] = jax.ShapeDtypeStruct((t, B_WIDTH), F32)
    out_shape += [jax.ShapeDtypeStruct((aug_t, AUG_WIDTH), BF16)] * 3
    out_specs += [aug_spec] * 3
    inputs = [x2d, *consts]
    in_specs = [row(D_MODEL)] + [full(a) for a in consts]
    aliases = {}
    if kv_prev is not None:
        for o, buf in zip(KV_OUT, kv_prev):
            aliases[len(inputs)] = o
            inputs.append(buf)
            in_specs.append(pl.BlockSpec(memory_space=pl.ANY))
    return pl.pallas_call(
        functools.partial(_inproj_kernel, layer, prompt, tiles_per_seq, precise, len(aliases)),
        grid=(t // tm,),
        in_specs=in_specs,
        out_specs=out_specs,
        out_shape=out_shape,
        input_output_aliases=aliases,
        scratch_shapes=[pltpu.VMEM((SUBLANES, LANES), F32)],
        compiler_params=pltpu.CompilerParams(
            dimension_semantics=("arbitrary",), vmem_limit_bytes=VMEM_LIMIT),
        name="inproj",
    )(*inputs)


def _hgrn_kernel(precise, q_ref, lf_ref, k_ref, i_ref, g_ref, st0_ref, cm_ref, mk_ref, e_ref,
                 gain_ref, o_ref, stout_ref, st_scr):
    mx = _Mx(precise)
    c = pl.program_id(1)

    @pl.when(c == 0)
    def _():
        st_scr[...] = st0_ref[...]

    if precise:
        p1, p2, p3 = _split3(lf_ref[...])
        r = _mm(cm_ref[...], jnp.concatenate([p1, p2, p3], axis=1))
        r = r[:, 0:256] + r[:, 256:512] + r[:, 512:768]
    else:
        p1, p2 = _split2(lf_ref[...])
        r = _mm(cm_ref[...], jnp.concatenate([p1, p2], axis=1))
        r = r[:, 0:256] + r[:, 256:512]
    q = q_ref[...]
    k = k_ref[...]
    iv = i_ref[...]
    b = r[0:CHUNK]
    su = r[CHUNK:2 * CHUNK]
    qd = q * jnp.exp(b)
    kd = k * jnp.exp(su)

    left = lax.broadcasted_iota(jnp.int32, (CHUNK, LANES), 1) < HEAD_DIM
    amat = [None] * A_HEADS
    for lv in range(len(HGRN_LEVELS)):
        ex = jnp.exp(r[(lv + 2) * CHUNK:(lv + 3) * CHUNK])
        ql = q * ex
        kl = k * ex
        m = mk_ref[lv]
        for pr in range(2):
            qp = ql[:, pr * LANES:(pr + 1) * LANES]
            kp = kl[:, pr * LANES:(pr + 1) * LANES]
            for hh in range(2):
                qm = jnp.where(left if hh == 0 else jnp.logical_not(left), qp, 0.0)
                s = m * mx.mm_nt(qm, kp)
                hd = 2 * pr + hh
                amat[hd] = s if amat[hd] is None else amat[hd] + s

    e256 = e_ref[...]
    o_pairs = []
    for pr in range(2):
        ip = iv[:, pr * LANES:(pr + 1) * LANES]
        o_pairs.append(jnp.where(left, mx.mm(amat[2 * pr], ip), mx.mm(amat[2 * pr + 1], ip)))
    o_intra = jnp.concatenate(o_pairs, axis=1)
    o_diag = _dot01_right(q * k, e256) * iv
    st = st_scr[...]
    o = mx.mm_nt(qd, st) + o_intra + o_diag
    o_ref[...] = (_group_rmsnorm(o, e256, gain_ref[...]) * g_ref[...]).astype(o_ref.dtype)

    upd = mx.mm(iv.T, kd)
    st_new = st * jnp.exp(b[CHUNK - 1:CHUNK, :]) + e256.astype(F32) * upd
    st_scr[...] = st_new

    @pl.when(c == pl.num_programs(1) - 1)
    def _():
        stout_ref[...] = st_new.T


def _hgrn(precise, qa, lfa, ka, ia, ga, st0, cm, mk, e256, gain, nb, nc):
    t = qa.shape[0]
    row = pl.BlockSpec((CHUNK, A_WIDTH), lambda b, c: (b * nc + c, 0))
    full = lambda a: pl.BlockSpec(a.shape, lambda b, c: (0,) * a.ndim)
    st_spec = pl.BlockSpec((None, A_WIDTH, A_WIDTH), lambda b, c: (b, 0, 0))
    return pl.pallas_call(
        functools.partial(_hgrn_kernel, precise),
        grid=(nb, nc),
        in_specs=[row, row, row, row, row, st_spec, full(cm), full(mk), full(e256), full(gain)],
        out_specs=[row, st_spec],
        out_shape=[jax.ShapeDtypeStruct((t, A_WIDTH), F32 if precise else BF16),
                   jax.ShapeDtypeStruct((nb, A_WIDTH, A_WIDTH), F32)],
        scratch_shapes=[pltpu.VMEM((A_WIDTH, A_WIDTH), F32)],
        compiler_params=pltpu.CompilerParams(
            dimension_semantics=("arbitrary", "arbitrary"), vmem_limit_bytes=VMEM_LIMIT),
        name="hgrn",
    )(qa, lfa, ka, ia, ga, st0, cm, mk, e256, gain)


def _fox_kernel(q_ref, k_ref, v_ref, gain_ref, wms_ref, sh_ref, o_ref, m_scr, acc_scr):
    i = pl.program_id(1)
    tq, tk = ATT_TQ, ATT_TK
    m_scr[...] = jnp.full(m_scr.shape, NEG_INF, F32)
    acc_scr[...] = jnp.zeros(acc_scr.shape, F32)
    n_full = (i * tq) // tk

    def block(r0, nr, ks, mask):
        rows = slice(r0, r0 + nr)
        for hd in range(B_HEADS):
            lanes = slice(hd * LANES, (hd + 1) * LANES)
            s = _mm_nt(q_ref[rows, lanes], k_ref[pl.ds(ks, tk), lanes])
            if mask is not None:
                s = jnp.where(mask, s, NEG_INF)
            m_prev = m_scr[hd, rows, :]
            m_new = jnp.maximum(m_prev, jnp.max(s, axis=-1, keepdims=True))
            alpha = jnp.exp2(m_prev - m_new)
            p = jnp.exp2(s - jnp.concatenate([m_new] * (tk // LANES), axis=1))
            acc_scr[hd, rows, :] = (alpha * acc_scr[hd, rows, :]
                                    + _mm(p.astype(BF16), v_ref[pl.ds(ks, tk), lanes]))
            m_scr[hd, rows, :] = m_new

    def body(j, carry):
        block(0, tq, pl.multiple_of(j * tk, tk), None)
        return carry

    lax.fori_loop(0, n_full, body, 0)
    tri = (lax.broadcasted_iota(jnp.int32, (tk, tk), 1)
           <= lax.broadcasted_iota(jnp.int32, (tk, tk), 0))
    for d in range(tq // tk):
        ks = pl.multiple_of((n_full + d) * tk, tk)
        block(d * tk, tk, ks, tri)
        if (d + 1) * tk < tq:
            block((d + 1) * tk, tq - (d + 1) * tk, ks, None)

    left = lax.broadcasted_iota(jnp.int32, (tq, 2 * LANES), 1) % LANES < HEAD_DIM
    for pr in range(B_HEADS // 2):
        a2 = jnp.concatenate([acc_scr[2 * pr], acc_scr[2 * pr + 1]], axis=1)
        y = a2 * a2
        y = jnp.where(left, y, y * (HEAD_DIM * RMS_EPS))
        yh, yl = _split2(y)
        z = _mm(yh, wms_ref[...]) + _mm(yl, wms_ref[...])
        on = a2 * lax.rsqrt(z * (1.0 / HEAD_DIM)) * gain_ref[:, 2 * pr * LANES:(2 * pr + 2) * LANES]
        onb = on.astype(BF16)
        pair = onb[:, 0:LANES].astype(F32) + _mm(onb[:, LANES:2 * LANES], sh_ref[...])
        o_ref[:, pr * LANES:(pr + 1) * LANES] = pair.astype(BF16)


def _fox(qaug, kaug, vaug, gain, wms, sh, nb, seq):
    t = qaug.shape[0]
    nq = seq // ATT_TQ
    resident = lambda: pl.BlockSpec((seq, AUG_WIDTH), lambda b, i: (b, 0), pipeline_mode=pl.Buffered(1))
    return pl.pallas_call(
        _fox_kernel,
        grid=(nb, nq),
        in_specs=[
            pl.BlockSpec((ATT_TQ, AUG_WIDTH), lambda b, i: (b * nq + i, 0)),
            resident(), resident(),
            pl.BlockSpec(gain.shape, lambda b, i: (0, 0)),
            pl.BlockSpec(wms.shape, lambda b, i: (0, 0)),
            pl.BlockSpec(sh.shape, lambda b, i: (0, 0)),
        ],
        out_specs=pl.BlockSpec((ATT_TQ, B_WIDTH), lambda b, i: (b * nq + i, 0)),
        out_shape=jax.ShapeDtypeStruct((t, B_WIDTH), BF16),
        scratch_shapes=[
            pltpu.VMEM((B_HEADS, ATT_TQ, LANES), F32),
            pltpu.VMEM((B_HEADS, ATT_TQ, LANES), F32),
        ],
        compiler_params=pltpu.CompilerParams(
            dimension_semantics=("arbitrary", "arbitrary"), vmem_limit_bytes=VMEM_LIMIT),
        name="fox_prompt",
    )(qaug, kaug, vaug, gain, wms, sh)


def _paged_kernel(layer, g_pages, pt_ref, q_ref, ck_hbm, cv_hbm, clf_hbm, kn_ref, vn_ref, lfn_ref,
                  triu_ref, bm_ref, nm_ref, gain_ref, o_ref,
                  m_scr, l_scr, acc_scr, carry_scr, kbuf, vbuf, lfbuf, sem):
    b = pl.program_id(0)
    p = pl.program_id(1)
    nb = pl.num_programs(0)
    last = pl.num_programs(1) - 1
    rows = q_ref.shape[0]
    flat = B_HEADS * HEAD_DIM
    slot = (b * last + p) % 2

    def page_copies(bb, pp, sl):
        cps = []
        for g in range(g_pages):
            page = pt_ref[bb, pp * g_pages + g]
            cps.append(pltpu.make_async_copy(ck_hbm.at[layer, page], kbuf.at[sl, g], sem.at[sl]))
            cps.append(pltpu.make_async_copy(cv_hbm.at[layer, page], vbuf.at[sl, g], sem.at[sl]))
            cps.append(pltpu.make_async_copy(clf_hbm.at[layer, page], lfbuf.at[sl, g], sem.at[sl]))
        return cps

    @pl.when((b == 0) & (p == 0))
    def _():
        for cp in page_copies(0, 0, 0):
            cp.start()

    @pl.when(p + 1 < last)
    def _():
        for cp in page_copies(b, p + 1, 1 - slot):
            cp.start()

    @pl.when((p + 1 == last) & (b + 1 < nb))
    def _():
        for cp in page_copies(b + 1, 0, 1 - slot):
            cp.start()

    @pl.when(p == 0)
    def _():
        m_scr[...] = jnp.full(m_scr.shape, NEG_INF, F32)
        l_scr[...] = jnp.zeros(l_scr.shape, F32)
        acc_scr[...] = jnp.zeros(acc_scr.shape, F32)
        carry_scr[...] = jnp.zeros(carry_scr.shape, F32)

    qh, ql = _split2(q_ref[...])
    q2 = jnp.concatenate([qh, ql], axis=0)

    def step(pages, mask):
        carry = carry_scr[...]
        ss = []
        for k_ref, _, lf_ref in pages:
            kh, kl = _split2(k_ref[...].reshape(flat, CHUNK))
            lf = jnp.concatenate([lf_ref[...], jnp.zeros((SUBLANES, CHUNK), F32)], axis=0)
            c = carry + _dot01_right(lf, triu_ref[...])[0:B_HEADS]
            carry = jnp.broadcast_to(c[:, CHUNK - 1:CHUNK], carry.shape)
            sk = _mm(q2, kh)
            s = sk[0:rows] + sk[rows:2 * rows] + _mm(qh, kl)
            s = (s.reshape(B_HEADS, SUBLANES, CHUNK) - c[:, None, :]).reshape(rows, CHUNK)
            if mask is not None:
                s = jnp.where(mask > 0.0, s, NEG_INF)
            ss.append(s)
        carry_scr[...] = carry
        s_all = ss[0] if len(ss) == 1 else jnp.concatenate(ss, axis=1)
        m_prev = m_scr[...]
        m_new = jnp.maximum(m_prev, jnp.max(s_all, axis=-1, keepdims=True))
        alpha = jnp.exp(m_prev - m_new)
        p_all = jnp.exp(s_all - jnp.concatenate([m_new] * len(ss), axis=1))
        l_scr[...] = alpha * l_scr[...] + jnp.sum(p_all, axis=-1, keepdims=True)
        m_scr[...] = m_new
        pv = None
        for g, (_, v_ref, _) in enumerate(pages):
            vh, vl = _split2(v_ref[...].reshape(flat, CHUNK))
            ph, plo = _split2(p_all[:, g * CHUNK:(g + 1) * CHUNK])
            pk = _mm_nt(jnp.concatenate([ph, plo], axis=0), vh)
            d = pk[0:rows] + pk[rows:2 * rows] + _mm_nt(ph, vl)
            pv = d if pv is None else pv + d
        acc_scr[...] = jnp.concatenate([alpha] * (flat // LANES), axis=1) * acc_scr[...] + pv

    @pl.when(p < last)
    def _():
        for cp in page_copies(b, p, slot):
            cp.wait()
        step([(kbuf.at[slot, g], vbuf.at[slot, g], lfbuf.at[slot, g]) for g in range(g_pages)], None)

    @pl.when(p == last)
    def _():
        step([(kn_ref, vn_ref, lfn_ref)], nm_ref[...])
        o = acc_scr[...] / jnp.concatenate([l_scr[...]] * (flat // LANES), axis=1)
        o = o * bm_ref[...]
        ms = jnp.sum(o * o, axis=-1, keepdims=True) * (1.0 / HEAD_DIM)
        on = o * lax.rsqrt(ms + RMS_EPS) * gain_ref[...]
        o_ref[...] = jnp.sum(on.reshape(B_HEADS, SUBLANES, flat), axis=0)


def _paged(layer, page_table, q_bd, ck, cv, clf, kn, vn, lfn, triu, bm, nm, gain):
    nb, n_pages = page_table.shape
    rows = q_bd.shape[1]
    g_pages = min(PAGES_PER_STEP, n_pages)
    assert n_pages % g_pages == 0
    n_steps = n_pages // g_pages

    kv_minor = (B_HEADS, HEAD_DIM, CHUNK)
    full2 = lambda a: pl.BlockSpec(a.shape, lambda b, p, pt: (0, 0))
    in_specs = [pl.BlockSpec((None, rows, B_WIDTH), lambda b, p, pt: (b, 0, 0))]
    in_specs += [pl.BlockSpec(memory_space=pl.ANY)] * 3
    in_specs += [
        pl.BlockSpec((None,) + kv_minor, lambda b, p, pt: (b, 0, 0, 0)),
        pl.BlockSpec((None,) + kv_minor, lambda b, p, pt: (b, 0, 0, 0)),
        pl.BlockSpec((None, B_HEADS, CHUNK), lambda b, p, pt: (b, 0, 0)),
        full2(triu), full2(bm), full2(nm), full2(gain),
    ]
    grid_spec = pltpu.PrefetchScalarGridSpec(
        num_scalar_prefetch=1,
        grid=(nb, n_steps + 1),
        in_specs=in_specs,
        out_specs=pl.BlockSpec((None, SUBLANES, B_WIDTH), lambda b, p, pt: (b, 0, 0)),
        scratch_shapes=[
            pltpu.VMEM((rows, LANES), F32),
            pltpu.VMEM((rows, LANES), F32),
            pltpu.VMEM((rows, B_WIDTH), F32),
            pltpu.VMEM((B_HEADS, CHUNK), F32),
            pltpu.VMEM((2, g_pages) + kv_minor, F32),
            pltpu.VMEM((2, g_pages) + kv_minor, F32),
            pltpu.VMEM((2, g_pages, B_HEADS, CHUNK), F32),
            pltpu.SemaphoreType.DMA((2,)),
        ],
    )
    return pl.pallas_call(
        functools.partial(_paged_kernel, layer, g_pages),
        grid_spec=grid_spec,
        out_shape=jax.ShapeDtypeStruct((nb, SUBLANES, B_WIDTH), F32),
        compiler_params=pltpu.CompilerParams(
            dimension_semantics=("arbitrary", "arbitrary"), vmem_limit_bytes=VMEM_LIMIT),
        name="fox_paged",
    )(page_table, q_bd, ck, cv, clf, kn, vn, lfn, triu, bm, nm, gain)


def _outproj_kernel(precise, x_ref, oa_ref, ob_ref, oc_ref, w_ref, g_ref, wrt_ref, brc_ref,
                    x2_ref, h2_ref, comb_ref):
    mx = _Mx(precise)
    tm = x_ref.shape[0]
    mix = (mx.mm(oa_ref[...], w_ref[0:A_WIDTH, :])
           + mx.mm(ob_ref[...], w_ref[A_WIDTH:A_WIDTH + B_WIDTH, :])
           + mx.mm(oc_ref[...], w_ref[A_WIDTH + B_WIDTH:, :]))
    x2 = x_ref[...] + mix
    x2_ref[...] = x2
    h2 = _rmsnorm_rows(x2, g_ref[...])
    h2b = h2.astype(BF16)
    h2_ref[...] = h2b
    h2l = (h2 - h2b.astype(F32)).astype(BF16)
    wr = wrt_ref[...]
    wrh = wr.astype(BF16)
    wrl = (wr - wrh.astype(F32)).astype(BF16)
    lg = _mm_nt(wrh, h2b) + _mm_nt(wrh, h2l) + _mm_nt(wrl, h2b) + brc_ref[...]
    row = [lg[i:i + 1, :] for i in range(N_GROUPS + N_EXPERTS)]

    g = row[0:N_GROUPS]
    gmax = jnp.maximum(jnp.maximum(g[0], g[1]), jnp.maximum(g[2], g[3]))
    gsel = jnp.where(g[0] == gmax, 0, jnp.where(g[1] == gmax, 1, jnp.where(g[2] == gmax, 2, 3)))
    den = (jnp.exp(g[0] - gmax) + jnp.exp(g[1] - gmax)) + (jnp.exp(g[2] - gmax) + jnp.exp(g[3] - gmax))
    gate = 1.0 / den
    le = []
    for e in range(N_EXP):
        le.append(jnp.where(gsel == 0, row[4 + e],
                            jnp.where(gsel == 1, row[8 + e],
                                      jnp.where(gsel == 2, row[12 + e], row[16 + e]))))
    v1 = jnp.maximum(jnp.maximum(le[0], le[1]), jnp.maximum(le[2], le[3]))
    i1 = jnp.where(le[0] == v1, 0, jnp.where(le[1] == v1, 1, jnp.where(le[2] == v1, 2, 3)))
    le2 = [jnp.where(i1 == e, -jnp.inf, le[e]) for e in range(N_EXP)]
    v2 = jnp.maximum(jnp.maximum(le2[0], le2[1]), jnp.maximum(le2[2], le2[3]))
    i2 = jnp.where(le2[0] == v2, 0, jnp.where(le2[1] == v2, 1, jnp.where(le2[2] == v2, 2, 3)))
    ex = jnp.exp(v2 - v1)
    w1 = 1.0 / (1.0 + ex)
    w2 = ex * w1
    rid = lax.broadcasted_iota(jnp.int32, (N_EXPERTS, tm), 0)
    combt = jnp.zeros((N_EXPERTS, tm), F32)
    for gi in range(N_GROUPS):
        for e in range(N_EXP):
            fine = jnp.where(i1 == e, w1, 0.0) + jnp.where(i2 == e, w2, 0.0)
            val = jnp.where(gsel == gi, gate * fine, 0.0)
            combt = jnp.where(rid == gi * N_EXP + e, val, combt)
    combt = jnp.concatenate([combt, jnp.zeros((LANES - N_EXPERTS, tm), F32)], axis=0)
    comb_ref[...] = combt.T


def _outproj(precise, x2d, oa, ob, oc, w_out, g, wrt, brc, tm):
    t = x2d.shape[0]
    row = lambda w: pl.BlockSpec((tm, w), lambda i: (i, 0))
    full = lambda a: pl.BlockSpec(a.shape, lambda i: (0,) * a.ndim)
    return pl.pallas_call(
        functools.partial(_outproj_kernel, precise),
        grid=(t // tm,),
        in_specs=[row(D_MODEL), row(A_WIDTH), row(B_WIDTH), row(C_WIDTH),
                  full(w_out), full(g), full(wrt), full(brc)],
        out_specs=[row(D_MODEL), row(D_MODEL), row(LANES)],
        out_shape=[jax.ShapeDtypeStruct((t, D_MODEL), F32),
                   jax.ShapeDtypeStruct((t, D_MODEL), BF16),
                   jax.ShapeDtypeStruct((t, LANES), F32)],
        compiler_params=pltpu.CompilerParams(
            dimension_semantics=("arbitrary",), vmem_limit_bytes=VMEM_LIMIT),
        name="outproj_router",
    )(x2d, oa, ob, oc, w_out, g, wrt, brc)


def _moe_kernel(final, x2_ref, h2_ref, comb_ref, wg_ref, wu_ref, wd_ref, gf_ref, y_ref, acc_scr):
    e = pl.program_id(1)

    @pl.when(e == 0)
    def _():
        acc_scr[...] = x2_ref[...]

    h = h2_ref[...]
    a = _mm(h, wg_ref[...])
    u = _mm(h, wu_ref[...])
    comb = comb_ref[...]
    lane = lax.broadcasted_iota(jnp.int32, comb.shape, 1)
    cw = jnp.sum(jnp.where(lane == e, comb, 0.0), axis=-1, keepdims=True)
    act = (a * _sigmoid(a)) * u * cw
    acc_scr[...] += _mm(act.astype(BF16), wd_ref[...])

    @pl.when(e == pl.num_programs(1) - 1)
    def _():
        y = acc_scr[...]
        if final:
            y = _rmsnorm_rows(y, gf_ref[...])
        y_ref[...] = y


def _moe(final, x2, h2, comb, wg, wu, wd, gf, tm):
    t = x2.shape[0]
    return pl.pallas_call(
        functools.partial(_moe_kernel, final),
        grid=(t // tm, N_EXPERTS),
        in_specs=[
            pl.BlockSpec((tm, D_MODEL), lambda i, e: (i, 0)),
            pl.BlockSpec((tm, D_MODEL), lambda i, e: (i, 0)),
            pl.BlockSpec((tm, LANES), lambda i, e: (i, 0)),
            pl.BlockSpec((None, D_MODEL, EXPERT_FF), lambda i, e: (e, 0, 0)),
            pl.BlockSpec((None, D_MODEL, EXPERT_FF), lambda i, e: (e, 0, 0)),
            pl.BlockSpec((None, EXPERT_FF, D_MODEL), lambda i, e: (e, 0, 0)),
            pl.BlockSpec(gf.shape, lambda i, e: (0, 0)),
        ],
        out_specs=pl.BlockSpec((tm, D_MODEL), lambda i, e: (i, 0)),
        out_shape=jax.ShapeDtypeStruct((t, D_MODEL), F32),
        scratch_shapes=[pltpu.VMEM((tm, D_MODEL), F32)],
        compiler_params=pltpu.CompilerParams(
            dimension_semantics=("arbitrary", "arbitrary"), vmem_limit_bytes=VMEM_LIMIT),
        name="moe",
    )(x2, h2, comb, wg, wu, wd, gf)


def _block_diag_state(s):
    n = s.shape[0]
    out = jnp.zeros((n, A_HEADS, HEAD_DIM, A_HEADS, HEAD_DIM), s.dtype)
    st = jnp.swapaxes(s, -1, -2)
    for h in range(A_HEADS):
        out = out.at[:, h, :, h, :].set(st[:, h])
    return out.reshape(n, A_WIDTH, A_WIDTH)


def _state_blocks(sbd):
    n = sbd.shape[0]
    s5 = sbd.reshape(n, A_HEADS, HEAD_DIM, A_HEADS, HEAD_DIM)
    return jnp.stack([s5[:, h, :, h, :] for h in range(A_HEADS)], axis=1)


def _pad_rows(a, nb, per, to):
    w = a.shape[-1]
    a3 = a.reshape(nb, per, w)
    return jnp.pad(a3, ((0, 0), (0, to - per), (0, 0))).reshape(nb * to, w)


def kernel(x_prompt, x_sample, cache_k, cache_v, cache_logf, state_hgrn, page_table, norm_mix_gain, w_in, hgrn_lb_logits, hgrn_out_gain, fox_f_bias, fox_out_gain, sgu_v_gain, sgu_w_s, sgu_b, sgu_out_gain, w_out, norm_ffn_gain, router_group_w, router_group_b, router_expert_w, router_expert_b, expert_w_gate, expert_w_up, expert_w_down, norm_final_gain):
    nb, seq, _ = x_prompt.shape
    ndb, dseq, _ = x_sample.shape
    depth = w_in.shape[0]
    tp = nb * seq
    ts = ndb * dseq
    tm_p = 256
    assert seq % ATT_TQ == 0 and ATT_TQ % ATT_TK == 0 and seq % tm_p == 0
    assert ts % SUBLANES == 0 and ts <= CHUNK and dseq <= SUBLANES

    cm_np, mk_np = _hgrn_consts()
    cm = jnp.asarray(cm_np, BF16)
    mk = jnp.asarray(mk_np, F32)
    e256 = jnp.asarray(_block_ones(A_WIDTH, HEAD_DIM), BF16)
    place_np, qone_np, vone_np = _aug_consts()
    place = jnp.asarray(place_np, BF16)
    wms_np = np.zeros((2 * LANES, 2 * LANES), np.float32)
    sh_np = np.zeros((LANES, LANES), np.float32)
    for blk in range(2):
        wms_np[blk * LANES:blk * LANES + HEAD_DIM + 1, blk * LANES:(blk + 1) * LANES] = 1.0
    sh_np[np.arange(HEAD_DIM), HEAD_DIM + np.arange(HEAD_DIM)] = 1.0
    wms = jnp.asarray(wms_np, BF16)
    sh = jnp.asarray(sh_np, BF16)
    qone = jnp.asarray(qone_np)
    vone = jnp.asarray(vone_np)
    ar = np.arange(tm_p)
    tril_tm = jnp.asarray((ar[None, :] <= ar[:, None]).astype(np.float32), BF16)
    ac = np.arange(CHUNK)
    triu_pg = jnp.asarray((ac[:, None] <= ac[None, :]).astype(np.float32), BF16)
    rows = B_HEADS * SUBLANES
    rr = np.arange(rows)
    col = np.arange(B_WIDTH)
    bm = jnp.asarray((rr[:, None] // SUBLANES == col[None, :] // HEAD_DIM).astype(np.float32))
    tq_of_row = rr[:, None] % SUBLANES
    nm = jnp.asarray(((ac[None, :] < dseq) & (ac[None, :] <= tq_of_row)
                      & (tq_of_row < dseq)).astype(np.float32))

    sr = np.arange(ts)
    same_seq = jnp.asarray(((sr[:, None] // dseq) == (sr[None, :] // dseq))
                           & ((sr[None, :] % dseq) <= (sr[:, None] % dseq)))
    tril = jnp.asarray(np.tril(np.ones((CHUNK, CHUNK), np.float32)))

    ck = jnp.transpose(cache_k, (0, 1, 3, 4, 2))
    cv = jnp.transpose(cache_v, (0, 1, 3, 4, 2))
    clf = jnp.transpose(cache_logf, (0, 1, 3, 2))

    xp = x_prompt.reshape(tp, D_MODEL)
    xs = x_sample.reshape(ts, D_MODEL)
    zeros_state = jnp.zeros((nb, A_WIDTH, A_WIDTH), F32)

    outs_p = {k: [] for k in ("lf", "s")}
    kv_p = None
    outs_s = {k: [] for k in ("k", "v", "lf", "s", "vn")}

    for l in range(depth):
        wl = w_in[l]
        w_r32 = jnp.concatenate(
            [wl[:, 0:2560], wl[:, 2568:3080], wl[:, 2560:2568],
             jnp.zeros((D_MODEL, IN_COLS_PAD - 3080), F32)], axis=1)
        w_r = w_r32.astype(BF16)
        fb = jnp.pad(fox_f_bias[l][None, :], ((0, 0), (0, LANES - B_HEADS)))
        g_mix = norm_mix_gain[l][None, :]
        vg = sgu_v_gain[l][None, :]
        sog = sgu_out_gain[l][None, :]
        smat_p = (sgu_w_s[l] * tril).astype(BF16)
        sbias_p = jnp.repeat(sgu_b[l].T, HEAD_DIM, axis=1)
        w4 = sgu_w_s[l][:, :dseq, :dseq]
        smat_s = jnp.where(same_seq, jnp.tile(w4, (1, ndb, ndb)), 0.0)
        sbias_s = jnp.repeat(jnp.tile(sgu_b[l][:, :dseq].T, (ndb, 1)), HEAD_DIM, axis=1)
        hg = hgrn_out_gain[l][None, :]
        fg = fox_out_gain[l][None, :]
        fg_blk = jnp.pad(fox_out_gain[l].reshape(B_HEADS, HEAD_DIM),
                         ((0, 0), (0, LANES - HEAD_DIM))).reshape(1, AUG_WIDTH)
        wo32 = w_out[l]
        wo = wo32.astype(BF16)
        g_ffn = norm_ffn_gain[l][None, :]
        wrt = jnp.concatenate(
            [router_group_w[l].T,
             jnp.transpose(router_expert_w[l], (0, 2, 1)).reshape(N_EXPERTS, D_MODEL),
             jnp.zeros((32 - N_GROUPS - N_EXPERTS, D_MODEL), F32)], axis=0)
        brc = jnp.concatenate(
            [router_group_b[l], router_expert_b[l].reshape(-1),
             jnp.zeros((32 - N_GROUPS - N_EXPERTS,), F32)])[:, None]
        wg = expert_w_gate[l].reshape(N_EXPERTS, D_MODEL, EXPERT_FF).astype(BF16)
        wu = expert_w_up[l].reshape(N_EXPERTS, D_MODEL, EXPERT_FF).astype(BF16)
        wd = expert_w_down[l].reshape(N_EXPERTS, EXPERT_FF, D_MODEL).astype(BF16)
        gf = norm_final_gain[None, :]
        final = l == depth - 1

        consts_p = (g_mix, w_r, hgrn_lb_logits, fb, vg, smat_p, sbias_p, sog, e256,
                    tril_tm, place, qone, vone)
        (qa, lfa, ka, ia, ga, _, kb, vb, lfb, oc, _, qaug, kaug, vaug) = _inproj(
            l, True, seq // tm_p, False, xp, consts_p, tm_p, depth, kv_p)
        kv_p = (kb, vb)
        oa, st = _hgrn(False, qa, lfa, ka, ia, ga, zeros_state, cm, mk, e256, hg, nb, seq // CHUNK)
        ob = _fox(qaug, kaug, vaug, fg_blk, wms, sh, nb, seq)
        x2, h2, comb = _outproj(False, xp, oa, ob, oc, wo, g_ffn, wrt, brc, tm_p)
        xp = _moe(final, x2, h2, comb, wg, wu, wd, gf, 1024 if tp % 1024 == 0 else tm_p)
        outs_p["lf"].append(lfb.reshape(nb, seq, B_HEADS))
        outs_p["s"].append(_state_blocks(st))

        consts_s = (g_mix, w_r32, hgrn_lb_logits, fb, vg, smat_s, sbias_s, sog, e256,
                    tril_tm, place, qone, vone)
        (qa, lfa, ka, ia, ga, qb, kb, vb, lfb, oc, vn, _, _, _) = _inproj(
            l, False, 1, True, xs, consts_s, ts, 1, None)
        pads = [_pad_rows(a, ndb, dseq, CHUNK) for a in (qa, lfa, ka, ia, ga)]
        oa_pad, st = _hgrn(True, *pads, _block_diag_state(state_hgrn[l]), cm, mk, e256, hg, ndb, 1)
        oa = oa_pad.reshape(ndb, CHUNK, A_WIDTH)[:, :dseq].reshape(ts, A_WIDTH)
        q4 = jnp.pad(qb.reshape(ndb, dseq, B_HEADS, HEAD_DIM),
                     ((0, 0), (0, SUBLANES - dseq), (0, 0), (0, 0)))
        q_bd = (jnp.transpose(q4, (0, 2, 1, 3))[:, :, :, None, :]
                * jnp.eye(B_HEADS, dtype=F32)[None, :, None, :, None]).reshape(ndb, rows, B_WIDTH)
        k5 = kb.reshape(ndb, dseq, B_HEADS, HEAD_DIM)
        v5 = vb.reshape(ndb, dseq, B_HEADS, HEAD_DIM)
        lf3 = lfb.reshape(ndb, dseq, B_HEADS)
        padn = ((0, 0), (0, 0), (0, 0), (0, CHUNK - dseq))
        ob8 = _paged(l, page_table, q_bd, ck, cv, clf,
                     jnp.pad(jnp.transpose(k5, (0, 2, 3, 1)), padn),
                     jnp.pad(jnp.transpose(v5, (0, 2, 3, 1)), padn),
                     jnp.pad(jnp.transpose(lf3, (0, 2, 1)), padn[1:]),
                     triu_pg, bm, nm, fg)
        ob = ob8[:, :dseq].reshape(ts, B_WIDTH)
        x2, h2, comb = _outproj(True, xs, oa, ob, oc, wo32, g_ffn, wrt, brc, ts)
        xs = _moe(final, x2, h2, comb, wg, wu, wd, gf, ts)
        outs_s["k"].append(k5)
        outs_s["v"].append(v5)
        outs_s["lf"].append(lf3)
        outs_s["s"].append(_state_blocks(st))
        outs_s["vn"].append(vn.reshape(ndb, dseq, C_WIDTH))

    return (xp.reshape(nb, seq, D_MODEL), xs.reshape(ndb, dseq, D_MODEL),
            jnp.transpose(kv_p[0].reshape(depth, nb, B_HEADS, HEAD_DIM, seq), (0, 1, 4, 2, 3)),
            jnp.transpose(kv_p[1].reshape(depth, nb, B_HEADS, HEAD_DIM, seq), (0, 1, 4, 2, 3)),
            jnp.stack(outs_p["lf"]),
            jnp.stack(outs_p["s"]),
            jnp.stack(outs_s["k"]), jnp.stack(outs_s["v"]), jnp.stack(outs_s["lf"]),
            jnp.stack(outs_s["s"]), jnp.stack(outs_s["vn"]))
```

```python
import functools

import numpy as np
import jax
import jax.numpy as jnp
from jax import lax
from jax.experimental import pallas as pl
from jax.experimental.pallas import tpu as pltpu

F32 = jnp.float32
BF16 = jnp.bfloat16

D_MODEL = 1024
HEAD_DIM = 64
A_HEADS = 4
A_WIDTH = A_HEADS * HEAD_DIM
B_HEADS = 8
B_WIDTH = B_HEADS * HEAD_DIM
C_GROUPS = 4
C_WIDTH = C_GROUPS * HEAD_DIM
CHUNK = 128
N_GROUPS = 4
N_EXP = 4
N_EXPERTS = N_GROUPS * N_EXP
EXPERT_FF = D_MODEL // 4
RMS_EPS = 1e-6
NEG_INF = -1e30
LOG2E = 1.4426950408889634
LANES = 128
SUBLANES = 8
IN_COLS_PAD = 3200
AUG_WIDTH = B_HEADS * LANES
HGRN_LEVELS = (1, 2, 4, 8, 16, 32, 64)
ATT_TQ = 512
ATT_TK = 256
PAGES_PER_STEP = 16
VMEM_LIMIT = 48 * 1024 * 1024


class _Mx:
    def __init__(self, precise):
        self.precise = precise
        self.prec = lax.Precision.HIGHEST if precise else None

    def cast(self, x):
        return x.astype(F32) if self.precise else x.astype(BF16)

    def mm(self, a, b):
        return jnp.dot(self.cast(a), self.cast(b), preferred_element_type=F32, precision=self.prec)

    def mm_nt(self, a, b):
        return lax.dot_general(self.cast(a), self.cast(b), (((1,), (1,)), ((), ())),
                               preferred_element_type=F32, precision=self.prec)


def _mm(a, b):
    return jnp.dot(a, b, preferred_element_type=F32)


def _mm_nt(a, b):
    return lax.dot_general(a, b, (((1,), (1,)), ((), ())), preferred_element_type=F32)


def _split3(x):
    p1 = x.astype(BF16)
    r1 = x - p1.astype(F32)
    p2 = r1.astype(BF16)
    r2 = r1 - p2.astype(F32)
    return p1, p2, r2.astype(BF16)


def _split2(x):
    hi = x.astype(BF16)
    return hi, (x - hi.astype(F32)).astype(BF16)


def _dot01_left(m01, x):
    p1, p2, p3 = _split3(x)
    return _mm(m01, p1) + _mm(m01, p2) + _mm(m01, p3)


def _dot01_right(x, m01):
    p1, p2, p3 = _split3(x)
    return _mm(p1, m01) + _mm(p2, m01) + _mm(p3, m01)


def _sigmoid(x):
    return 1.0 / (1.0 + jnp.exp(-x))


def _log_sigmoid(x):
    return jnp.minimum(x, 0.0) - jnp.log1p(jnp.exp(-jnp.abs(x)))


def _gelu_tanh(x):
    return x * (0.5 * (1.0 + jnp.tanh(0.7978845608028654 * (x + 0.044715 * (x * x * x)))))


def _rmsnorm_rows(x, g):
    return x * lax.rsqrt(jnp.mean(x * x, axis=-1, keepdims=True) + RMS_EPS) * g


def _group_rmsnorm(x, e01, g):
    ms = _dot01_right(x * x, e01) * (1.0 / HEAD_DIM)
    return x * lax.rsqrt(ms + RMS_EPS) * g


def _head_blocks(x):
    rows = x.shape[0]
    left = lax.broadcasted_iota(jnp.int32, (rows, LANES), 1) < HEAD_DIM
    out = []
    for pr in range(B_HEADS // 2):
        xp = x[:, pr * LANES:(pr + 1) * LANES]
        out.append(jnp.where(left, xp, 0.0))
        out.append(jnp.where(left, pltpu.roll(xp, HEAD_DIM, 1), 0.0))
    return jnp.concatenate(out, axis=1)


def _block_ones(n, blk):
    i = np.arange(n)
    return (i[:, None] // blk == i[None, :] // blk).astype(np.float32)


def _hgrn_consts():
    n = CHUNK
    t = np.arange(n)[:, None]
    j = np.arange(n)[None, :]
    mats = [j <= t, j > t]
    masks = []
    for h in HGRN_LEVELS:
        same = (t // (2 * h)) == (j // (2 * h))
        off_j = j % (2 * h)
        off_t = t % (2 * h)
        mats.append((same & (off_j >= h) & (j <= t)) | (same & (off_j < h) & (j > t)))
        masks.append(same & (off_t >= h) & (off_j < h))
    cm = np.concatenate([m.astype(np.float32) for m in mats], axis=0)
    mk = np.stack([m.astype(np.float32) for m in masks])
    return cm, mk


def _aug_consts():
    place = np.zeros((3 * LANES, AUG_WIDTH), np.float32)
    qone = np.zeros((1, AUG_WIDTH), np.float32)
    vone = np.zeros((1, AUG_WIDTH), np.float32)
    for h in range(B_HEADS):
        for part in range(3):
            place[part * LANES + h, h * LANES + HEAD_DIM + part] = 1.0
            qone[0, h * LANES + HEAD_DIM + part] = 1.0
        vone[0, h * LANES + HEAD_DIM:(h + 1) * LANES] = 1.0
    return place, qone, vone


def _inproj_kernel(layer, prompt, tiles_per_seq, precise, n_aliased,
                   x_ref, g_ref, w_ref, lbl_ref, fb_ref, vg_ref, smat_ref, sbias_ref, sog_ref,
                   e_ref, tril_ref, place_ref, qone_ref, vone_ref, *rest):
    (qa_ref, lfa_ref, ka_ref, ia_ref, ga_ref, qb_ref, kb_ref, vb_ref, lfb_ref,
     oc_ref, vn_ref, qaug_ref, kaug_ref, vaug_ref, carry_scr) = rest[n_aliased:]
    mx = _Mx(precise)
    tm = x_ref.shape[0]
    h = _rmsnorm_rows(x_ref[...], g_ref[...])
    y = mx.mm(h, w_ref[...])
    a_q = y[:, 0:256]
    z = y[:, 256:512]
    a_i = y[:, 512:768]
    a_g = y[:, 768:1024]
    b_q = y[:, 1024:1536] * (HEAD_DIM ** -0.5)
    b_k = y[:, 1536:2048]
    b_v = y[:, 2048:2560]
    c_u = y[:, 2560:2816]
    c_v = y[:, 2816:3072]
    b_f = y[:, 3072:3200]

    lbl = lbl_ref[...]
    ex = jnp.exp(lbl - jnp.max(lbl, axis=0, keepdims=True))
    p = ex / jnp.sum(ex, axis=0, keepdims=True)
    cs = p[0:1]
    for j in range(1, layer + 1):
        cs = cs + p[j:j + 1]
    lb = cs - p[0:1]
    la = jnp.log(lb)
    lbb = jnp.log1p(-lb) + _log_sigmoid(z)
    lfa_ref[...] = jnp.maximum(la, lbb) + jnp.log1p(jnp.exp(-jnp.abs(la - lbb)))
    ka_ref[...] = (1.0 - lb) * _sigmoid(-z)
    qa_ref[...] = a_q * _sigmoid(a_q)
    ia_ref[...] = a_i
    ga_ref[...] = a_g * _sigmoid(a_g)

    qb_ref[...] = b_q
    if prompt:
        kb_ref[...] = b_k.T
        vb_ref[...] = b_v.T
    else:
        kb_ref[...] = b_k
        vb_ref[...] = b_v
    lfb = _log_sigmoid(b_f + fb_ref[...])
    lfb_ref[...] = lfb[:, 0:B_HEADS]

    if prompt:
        i = pl.program_id(0)

        @pl.when(i % tiles_per_seq == 0)
        def _():
            carry_scr[...] = jnp.zeros(carry_scr.shape, F32)

        c = carry_scr[0:1, :] + _dot01_left(tril_ref[...], lfb)
        carry_scr[...] = jnp.broadcast_to(c[tm - 1:tm, :], carry_scr.shape)
        n1, n2, n3 = _split3(c * (-LOG2E))
        bias = _mm(jnp.concatenate([n1, n2, n3], axis=1), place_ref[...])
        qaug_ref[...] = (_head_blocks(b_q * LOG2E) + qone_ref[...]).astype(BF16)
        kaug_ref[...] = (_head_blocks(b_k) + bias).astype(BF16)
        vaug_ref[...] = (_head_blocks(b_v) + vone_ref[...]).astype(BF16)
    else:
        qaug_ref[...] = jnp.zeros(qaug_ref.shape, BF16)
        kaug_ref[...] = jnp.zeros(kaug_ref.shape, BF16)
        vaug_ref[...] = jnp.zeros(vaug_ref.shape, BF16)

    e256 = e_ref[...]
    u = _gelu_tanh(c_u)
    vn = _group_rmsnorm(_gelu_tanh(c_v), e256, vg_ref[...])
    vn_ref[...] = vn
    left = lax.broadcasted_iota(jnp.int32, (CHUNK, LANES), 1) < HEAD_DIM
    zrows = []
    for c in range(tm // CHUNK):
        zp = []
        for pr in range(2):
            vp = vn[c * CHUNK:(c + 1) * CHUNK, pr * LANES:(pr + 1) * LANES]
            zp.append(jnp.where(left, mx.mm(smat_ref[2 * pr], vp), mx.mm(smat_ref[2 * pr + 1], vp)))
        zrows.append(jnp.concatenate(zp, axis=1) + sbias_ref[...])
    zc = zrows[0] if len(zrows) == 1 else jnp.concatenate(zrows, axis=0)
    oc_ref[...] = _group_rmsnorm(u * zc, e256, sog_ref[...]).astype(oc_ref.dtype)


KV_OUT = (6, 7)


def _inproj(layer, prompt, tiles_per_seq, precise, x2d, consts, tm, kv_depth, kv_prev):
    t = x2d.shape[0]
    row = lambda w: pl.BlockSpec((tm, w), lambda i: (i, 0))
    full = lambda a: pl.BlockSpec(a.shape, lambda i: (0,) * a.ndim)
    aug_rows = tm if prompt else SUBLANES
    aug_t = t if prompt else SUBLANES
    aug_spec = pl.BlockSpec((aug_rows, AUG_WIDTH), (lambda i: (i, 0)) if prompt else (lambda i: (0, 0)))
    outs = [(A_WIDTH, F32)] * 5 + [(B_WIDTH, F32)] * 3 + [(B_HEADS, F32), (C_WIDTH, F32 if precise else BF16), (C_WIDTH, F32)]
    out_shape = [jax.ShapeDtypeStruct((t, w), d) for w, d in outs]
    out_specs = [row(w) for w, _ in outs]
    for o in KV_OUT:
        if prompt:
            seq = tiles_per_seq * tm
            out_shape[o] = jax.ShapeDtypeStruct((kv_depth, t // seq, B_WIDTH, seq), F32)
            out_specs[o] = pl.BlockSpec(
                (None, None, B_WIDTH, tm),
                lambda i: (layer, i // tiles_per_seq, 0, i % tiles_per_seq))
        else:
            out_shape[o] = jax.ShapeDtypeStruct((t, B_WIDTH), F32)
    out_shape += [jax.ShapeDtypeStruct((aug_t, AUG_WIDTH), BF16)] * 3
    out_specs += [aug_spec] * 3
    inputs = [x2d, *consts]
    in_specs = [row(D_MODEL)] + [full(a) for a in consts]
    aliases = {}
    if kv_prev is not None:
        for o, buf in zip(KV_OUT, kv_prev):
            aliases[len(inputs)] = o
            inputs.append(buf)
            in_specs.append(pl.BlockSpec(memory_space=pl.ANY))
    return pl.pallas_call(
        functools.partial(_inproj_kernel, layer, prompt, tiles_per_seq, precise, len(aliases)),
        grid=(t // tm,),
        in_specs=in_specs,
        out_specs=out_specs,
        out_shape=out_shape,
        input_output_aliases=aliases,
        scratch_shapes=[pltpu.VMEM((SUBLANES, LANES), F32)],
        compiler_params=pltpu.CompilerParams(
            dimension_semantics=("arbitrary",), vmem_limit_bytes=VMEM_LIMIT),
        name="inproj",
    )(*inputs)


def _hgrn_kernel(precise, n_levels, q_ref, lf_ref, k_ref, i_ref, g_ref, st0_ref, cm_ref, mk_ref, e_ref,
                 gain_ref, o_ref, stout_ref, st_scr):
    mx = _Mx(precise)
    c = pl.program_id(1)

    @pl.when(c == 0)
    def _():
        st_scr[...] = st0_ref[...]

    cmat = cm_ref[0:(2 + n_levels) * CHUNK, :]
    if precise:
        p1, p2, p3 = _split3(lf_ref[...])
        r = _mm(cmat, jnp.concatenate([p1, p2, p3], axis=1))
        r = r[:, 0:256] + r[:, 256:512] + r[:, 512:768]
    else:
        p1, p2 = _split2(lf_ref[...])
        r = _mm(cmat, jnp.concatenate([p1, p2], axis=1))
        r = r[:, 0:256] + r[:, 256:512]
    q = q_ref[...]
    k = k_ref[...]
    iv = i_ref[...]
    b = r[0:CHUNK]
    su = r[CHUNK:2 * CHUNK]
    qd = q * jnp.exp(b)
    kd = k * jnp.exp(su)

    left = lax.broadcasted_iota(jnp.int32, (CHUNK, LANES), 1) < HEAD_DIM
    amat = [None] * A_HEADS
    for lv in range(n_levels):
        ex = jnp.exp(r[(lv + 2) * CHUNK:(lv + 3) * CHUNK])
        ql = q * ex
        kl = k * ex
        m = mk_ref[lv]
        for pr in range(2):
            qp = ql[:, pr * LANES:(pr + 1) * LANES]
            kp = kl[:, pr * LANES:(pr + 1) * LANES]
            for hh in range(2):
                qm = jnp.where(left if hh == 0 else jnp.logical_not(left), qp, 0.0)
                s = m * mx.mm_nt(qm, kp)
                hd = 2 * pr + hh
                amat[hd] = s if amat[hd] is None else amat[hd] + s

    e256 = e_ref[...]
    o_pairs = []
    for pr in range(2):
        ip = iv[:, pr * LANES:(pr + 1) * LANES]
        o_pairs.append(jnp.where(left, mx.mm(amat[2 * pr], ip), mx.mm(amat[2 * pr + 1], ip)))
    o_intra = jnp.concatenate(o_pairs, axis=1)
    o_diag = _dot01_right(q * k, e256) * iv
    st = st_scr[...]
    o = mx.mm_nt(qd, st) + o_intra + o_diag
    o_ref[...] = (_group_rmsnorm(o, e256, gain_ref[...]) * g_ref[...]).astype(o_ref.dtype)

    upd = mx.mm(iv.T, kd)
    st_new = st * jnp.exp(b[CHUNK - 1:CHUNK, :]) + e256.astype(F32) * upd
    st_scr[...] = st_new

    @pl.when(c == pl.num_programs(1) - 1)
    def _():
        stout_ref[...] = st_new.T


def _hgrn(precise, n_levels, qa, lfa, ka, ia, ga, st0, cm, mk, e256, gain, nb, nc):
    t = qa.shape[0]
    row = pl.BlockSpec((CHUNK, A_WIDTH), lambda b, c: (b * nc + c, 0))
    full = lambda a: pl.BlockSpec(a.shape, lambda b, c: (0,) * a.ndim)
    st_spec = pl.BlockSpec((None, A_WIDTH, A_WIDTH), lambda b, c: (b, 0, 0))
    return pl.pallas_call(
        functools.partial(_hgrn_kernel, precise, n_levels),
        grid=(nb, nc),
        in_specs=[row, row, row, row, row, st_spec, full(cm), full(mk), full(e256), full(gain)],
        out_specs=[row, st_spec],
        out_shape=[jax.ShapeDtypeStruct((t, A_WIDTH), F32 if precise else BF16),
                   jax.ShapeDtypeStruct((nb, A_WIDTH, A_WIDTH), F32)],
        scratch_shapes=[pltpu.VMEM((A_WIDTH, A_WIDTH), F32)],
        compiler_params=pltpu.CompilerParams(
            dimension_semantics=("arbitrary", "arbitrary"), vmem_limit_bytes=VMEM_LIMIT),
        name="hgrn",
    )(qa, lfa, ka, ia, ga, st0, cm, mk, e256, gain)


def _fox_kernel(q_ref, k_ref, v_ref, gain_ref, wms_ref, sh_ref, o_ref, m_scr, acc_scr):
    i = pl.program_id(1)
    tq, tk = ATT_TQ, ATT_TK
    m_scr[...] = jnp.full(m_scr.shape, NEG_INF, F32)
    acc_scr[...] = jnp.zeros(acc_scr.shape, F32)
    n_full = (i * tq) // tk

    def block(r0, nr, ks, mask):
        rows = slice(r0, r0 + nr)
        for hd in range(B_HEADS):
            lanes = slice(hd * LANES, (hd + 1) * LANES)
            s = _mm_nt(q_ref[rows, lanes], k_ref[pl.ds(ks, tk), lanes])
            if mask is not None:
                s = jnp.where(mask, s, NEG_INF)
            m_prev = m_scr[hd, rows, :]
            m_new = jnp.maximum(m_prev, jnp.max(s, axis=-1, keepdims=True))
            alpha = jnp.exp2(m_prev - m_new)
            p = jnp.exp2(s - jnp.concatenate([m_new] * (tk // LANES), axis=1))
            acc_scr[hd, rows, :] = (alpha * acc_scr[hd, rows, :]
                                    + _mm(p.astype(BF16), v_ref[pl.ds(ks, tk), lanes]))
            m_scr[hd, rows, :] = m_new

    def body(j, carry):
        block(0, tq, pl.multiple_of(j * tk, tk), None)
        return carry

    lax.fori_loop(0, n_full, body, 0)
    tri = (lax.broadcasted_iota(jnp.int32, (tk, tk), 1)
           <= lax.broadcasted_iota(jnp.int32, (tk, tk), 0))
    for d in range(tq // tk):
        ks = pl.multiple_of((n_full + d) * tk, tk)
        block(d * tk, tk, ks, tri)
        if (d + 1) * tk < tq:
            block((d + 1) * tk, tq - (d + 1) * tk, ks, None)

    left = lax.broadcasted_iota(jnp.int32, (tq, 2 * LANES), 1) % LANES < HEAD_DIM
    for pr in range(B_HEADS // 2):
        a2 = jnp.concatenate([acc_scr[2 * pr], acc_scr[2 * pr + 1]], axis=1)
        y = a2 * a2
        y = jnp.where(left, y, y * (HEAD_DIM * RMS_EPS))
        yh, yl = _split2(y)
        z = _mm(yh, wms_ref[...]) + _mm(yl, wms_ref[...])
        on = a2 * lax.rsqrt(z * (1.0 / HEAD_DIM)) * gain_ref[:, 2 * pr * LANES:(2 * pr + 2) * LANES]
        onb = on.astype(BF16)
        pair = onb[:, 0:LANES].astype(F32) + _mm(onb[:, LANES:2 * LANES], sh_ref[...])
        o_ref[:, pr * LANES:(pr + 1) * LANES] = pair.astype(BF16)


def _fox(qaug, kaug, vaug, gain, wms, sh, nb, seq):
    t = qaug.shape[0]
    nq = seq // ATT_TQ
    resident = lambda: pl.BlockSpec((seq, AUG_WIDTH), lambda b, i: (b, 0), pipeline_mode=pl.Buffered(1))
    return pl.pallas_call(
        _fox_kernel,
        grid=(nb, nq),
        in_specs=[
            pl.BlockSpec((ATT_TQ, AUG_WIDTH), lambda b, i: (b * nq + i, 0)),
            resident(), resident(),
            pl.BlockSpec(gain.shape, lambda b, i: (0, 0)),
            pl.BlockSpec(wms.shape, lambda b, i: (0, 0)),
            pl.BlockSpec(sh.shape, lambda b, i: (0, 0)),
        ],
        out_specs=pl.BlockSpec((ATT_TQ, B_WIDTH), lambda b, i: (b * nq + i, 0)),
        out_shape=jax.ShapeDtypeStruct((t, B_WIDTH), BF16),
        scratch_shapes=[
            pltpu.VMEM((B_HEADS, ATT_TQ, LANES), F32),
            pltpu.VMEM((B_HEADS, ATT_TQ, LANES), F32),
        ],
        compiler_params=pltpu.CompilerParams(
            dimension_semantics=("arbitrary", "arbitrary"), vmem_limit_bytes=VMEM_LIMIT),
        name="fox_prompt",
    )(qaug, kaug, vaug, gain, wms, sh)


def _paged_kernel(layer, g_pages, pt_ref, q_ref, ck_hbm, cv_hbm, clf_hbm, kn_ref, vn_ref, lfn_ref,
                  triu_ref, bm_ref, nm_ref, gain_ref, o_ref,
                  m_scr, l_scr, acc_scr, carry_scr, kbuf, vbuf, lfbuf, sem):
    b = pl.program_id(0)
    p = pl.program_id(1)
    nb = pl.num_programs(0)
    last = pl.num_programs(1) - 1
    rows = q_ref.shape[0]
    flat = B_HEADS * HEAD_DIM
    slot = (b * last + p) % 2

    def page_copies(bb, pp, sl):
        cps = []
        for g in range(g_pages):
            page = pt_ref[bb, pp * g_pages + g]
            cps.append(pltpu.make_async_copy(ck_hbm.at[layer, page], kbuf.at[sl, g], sem.at[sl]))
            cps.append(pltpu.make_async_copy(cv_hbm.at[layer, page], vbuf.at[sl, g], sem.at[sl]))
            cps.append(pltpu.make_async_copy(clf_hbm.at[layer, page], lfbuf.at[sl, g], sem.at[sl]))
        return cps

    @pl.when((b == 0) & (p == 0))
    def _():
        for cp in page_copies(0, 0, 0):
            cp.start()

    @pl.when(p + 1 < last)
    def _():
        for cp in page_copies(b, p + 1, 1 - slot):
            cp.start()

    @pl.when((p + 1 == last) & (b + 1 < nb))
    def _():
        for cp in page_copies(b + 1, 0, 1 - slot):
            cp.start()

    @pl.when(p == 0)
    def _():
        m_scr[...] = jnp.full(m_scr.shape, NEG_INF, F32)
        l_scr[...] = jnp.zeros(l_scr.shape, F32)
        acc_scr[...] = jnp.zeros(acc_scr.shape, F32)
        carry_scr[...] = jnp.zeros(carry_scr.shape, F32)

    qh, ql = _split2(q_ref[...])
    q2 = jnp.concatenate([qh, ql], axis=0)

    def step(pages, mask):
        carry = carry_scr[...]
        ss = []
        for k_ref, _, lf_ref in pages:
            kh, kl = _split2(k_ref[...].reshape(flat, CHUNK))
            lf = jnp.concatenate([lf_ref[...], jnp.zeros((SUBLANES, CHUNK), F32)], axis=0)
            c = carry + _dot01_right(lf, triu_ref[...])[0:B_HEADS]
            carry = jnp.broadcast_to(c[:, CHUNK - 1:CHUNK], carry.shape)
            sk = _mm(q2, kh)
            s = sk[0:rows] + sk[rows:2 * rows] + _mm(qh, kl)
            s = (s.reshape(B_HEADS, SUBLANES, CHUNK) - c[:, None, :]).reshape(rows, CHUNK)
            if mask is not None:
                s = jnp.where(mask > 0.0, s, NEG_INF)
            ss.append(s)
        carry_scr[...] = carry
        s_all = ss[0] if len(ss) == 1 else jnp.concatenate(ss, axis=1)
        m_prev = m_scr[...]
        m_new = jnp.maximum(m_prev, jnp.max(s_all, axis=-1, keepdims=True))
        alpha = jnp.exp(m_prev - m_new)
        p_all = jnp.exp(s_all - jnp.concatenate([m_new] * len(ss), axis=1))
        l_scr[...] = alpha * l_scr[...] + jnp.sum(p_all, axis=-1, keepdims=True)
        m_scr[...] = m_new
        pv = None
        for g, (_, v_ref, _) in enumerate(pages):
            vh, vl = _split2(v_ref[...].reshape(flat, CHUNK))
            ph, plo = _split2(p_all[:, g * CHUNK:(g + 1) * CHUNK])
            pk = _mm_nt(jnp.concatenate([ph, plo], axis=0), vh)
            d = pk[0:rows] + pk[rows:2 * rows] + _mm_nt(ph, vl)
            pv = d if pv is None else pv + d
        acc_scr[...] = jnp.concatenate([alpha] * (flat // LANES), axis=1) * acc_scr[...] + pv

    @pl.when(p < last)
    def _():
        for cp in page_copies(b, p, slot):
            cp.wait()
        step([(kbuf.at[slot, g], vbuf.at[slot, g], lfbuf.at[slot, g]) for g in range(g_pages)], None)

    @pl.when(p == last)
    def _():
        step([(kn_ref, vn_ref, lfn_ref)], nm_ref[...])
        o = acc_scr[...] / jnp.concatenate([l_scr[...]] * (flat // LANES), axis=1)
        o = o * bm_ref[...]
        ms = jnp.sum(o * o, axis=-1, keepdims=True) * (1.0 / HEAD_DIM)
        on = o * lax.rsqrt(ms + RMS_EPS) * gain_ref[...]
        o_ref[...] = jnp.sum(on.reshape(B_HEADS, SUBLANES, flat), axis=0)


def _paged(layer, page_table, q_bd, ck, cv, clf, kn, vn, lfn, triu, bm, nm, gain):
    nb, n_pages = page_table.shape
    rows = q_bd.shape[1]
    g_pages = min(PAGES_PER_STEP, n_pages)
    assert n_pages % g_pages == 0
    n_steps = n_pages // g_pages

    kv_minor = (B_HEADS, HEAD_DIM, CHUNK)
    full2 = lambda a: pl.BlockSpec(a.shape, lambda b, p, pt: (0, 0))
    in_specs = [pl.BlockSpec((None, rows, B_WIDTH), lambda b, p, pt: (b, 0, 0))]
    in_specs += [pl.BlockSpec(memory_space=pl.ANY)] * 3
    in_specs += [
        pl.BlockSpec((None,) + kv_minor, lambda b, p, pt: (b, 0, 0, 0)),
        pl.BlockSpec((None,) + kv_minor, lambda b, p, pt: (b, 0, 0, 0)),
        pl.BlockSpec((None, B_HEADS, CHUNK), lambda b, p, pt: (b, 0, 0)),
        full2(triu), full2(bm), full2(nm), full2(gain),
    ]
    grid_spec = pltpu.PrefetchScalarGridSpec(
        num_scalar_prefetch=1,
        grid=(nb, n_steps + 1),
        in_specs=in_specs,
        out_specs=pl.BlockSpec((None, SUBLANES, B_WIDTH), lambda b, p, pt: (b, 0, 0)),
        scratch_shapes=[
            pltpu.VMEM((rows, LANES), F32),
            pltpu.VMEM((rows, LANES), F32),
            pltpu.VMEM((rows, B_WIDTH), F32),
            pltpu.VMEM((B_HEADS, CHUNK), F32),
            pltpu.VMEM((2, g_pages) + kv_minor, F32),
            pltpu.VMEM((2, g_pages) + kv_minor, F32),
            pltpu.VMEM((2, g_pages, B_HEADS, CHUNK), F32),
            pltpu.SemaphoreType.DMA((2,)),
        ],
    )
    return pl.pallas_call(
        functools.partial(_paged_kernel, layer, g_pages),
        grid_spec=grid_spec,
        out_shape=jax.ShapeDtypeStruct((nb, SUBLANES, B_WIDTH), F32),
        compiler_params=pltpu.CompilerParams(
            dimension_semantics=("arbitrary", "arbitrary"), vmem_limit_bytes=VMEM_LIMIT),
        name="fox_paged",
    )(page_table, q_bd, ck, cv, clf, kn, vn, lfn, triu, bm, nm, gain)


def _outproj_kernel(precise, x_ref, oa_ref, ob_ref, oc_ref, w_ref, g_ref, wrt_ref, brc_ref,
                    x2_ref, h2_ref, comb_ref):
    mx = _Mx(precise)
    tm = x_ref.shape[0]
    mix = (mx.mm(oa_ref[...], w_ref[0:A_WIDTH, :])
           + mx.mm(ob_ref[...], w_ref[A_WIDTH:A_WIDTH + B_WIDTH, :])
           + mx.mm(oc_ref[...], w_ref[A_WIDTH + B_WIDTH:, :]))
    x2 = x_ref[...] + mix
    x2_ref[...] = x2
    h2 = _rmsnorm_rows(x2, g_ref[...])
    h2b = h2.astype(BF16)
    h2_ref[...] = h2b
    h2l = (h2 - h2b.astype(F32)).astype(BF16)
    wr = wrt_ref[...]
    wrh = wr.astype(BF16)
    wrl = (wr - wrh.astype(F32)).astype(BF16)
    lg = _mm_nt(wrh, h2b) + _mm_nt(wrh, h2l) + _mm_nt(wrl, h2b) + brc_ref[...]
    row = [lg[i:i + 1, :] for i in range(N_GROUPS + N_EXPERTS)]

    g = row[0:N_GROUPS]
    gmax = jnp.maximum(jnp.maximum(g[0], g[1]), jnp.maximum(g[2], g[3]))
    gsel = jnp.where(g[0] == gmax, 0, jnp.where(g[1] == gmax, 1, jnp.where(g[2] == gmax, 2, 3)))
    den = (jnp.exp(g[0] - gmax) + jnp.exp(g[1] - gmax)) + (jnp.exp(g[2] - gmax) + jnp.exp(g[3] - gmax))
    gate = 1.0 / den
    le = []
    for e in range(N_EXP):
        le.append(jnp.where(gsel == 0, row[4 + e],
                            jnp.where(gsel == 1, row[8 + e],
                                      jnp.where(gsel == 2, row[12 + e], row[16 + e]))))
    v1 = jnp.maximum(jnp.maximum(le[0], le[1]), jnp.maximum(le[2], le[3]))
    i1 = jnp.where(le[0] == v1, 0, jnp.where(le[1] == v1, 1, jnp.where(le[2] == v1, 2, 3)))
    le2 = [jnp.where(i1 == e, -jnp.inf, le[e]) for e in range(N_EXP)]
    v2 = jnp.maximum(jnp.maximum(le2[0], le2[1]), jnp.maximum(le2[2], le2[3]))
    i2 = jnp.where(le2[0] == v2, 0, jnp.where(le2[1] == v2, 1, jnp.where(le2[2] == v2, 2, 3)))
    ex = jnp.exp(v2 - v1)
    w1 = 1.0 / (1.0 + ex)
    w2 = ex * w1
    rid = lax.broadcasted_iota(jnp.int32, (N_EXPERTS, tm), 0)
    combt = jnp.zeros((N_EXPERTS, tm), F32)
    for gi in range(N_GROUPS):
        for e in range(N_EXP):
            fine = jnp.where(i1 == e, w1, 0.0) + jnp.where(i2 == e, w2, 0.0)
            val = jnp.where(gsel == gi, gate * fine, 0.0)
            combt = jnp.where(rid == gi * N_EXP + e, val, combt)
    combt = jnp.concatenate([combt, jnp.zeros((LANES - N_EXPERTS, tm), F32)], axis=0)
    comb_ref[...] = combt.T


def _outproj(precise, x2d, oa, ob, oc, w_out, g, wrt, brc, tm):
    t = x2d.shape[0]
    row = lambda w: pl.BlockSpec((tm, w), lambda i: (i, 0))
    full = lambda a: pl.BlockSpec(a.shape, lambda i: (0,) * a.ndim)
    return pl.pallas_call(
        functools.partial(_outproj_kernel, precise),
        grid=(t // tm,),
        in_specs=[row(D_MODEL), row(A_WIDTH), row(B_WIDTH), row(C_WIDTH),
                  full(w_out), full(g), full(wrt), full(brc)],
        out_specs=[row(D_MODEL), row(D_MODEL), row(LANES)],
        out_shape=[jax.ShapeDtypeStruct((t, D_MODEL), F32),
                   jax.ShapeDtypeStruct((t, D_MODEL), BF16),
                   jax.ShapeDtypeStruct((t, LANES), F32)],
        compiler_params=pltpu.CompilerParams(
            dimension_semantics=("arbitrary",), vmem_limit_bytes=VMEM_LIMIT),
        name="outproj_router",
    )(x2d, oa, ob, oc, w_out, g, wrt, brc)


def _moe_kernel(final, x2_ref, h2_ref, comb_ref, wg_ref, wu_ref, wd_ref, gf_ref, y_ref, acc_scr):
    e = pl.program_id(1)

    @pl.when(e == 0)
    def _():
        acc_scr[...] = x2_ref[...]

    h = h2_ref[...]
    a = _mm(h, wg_ref[...])
    u = _mm(h, wu_ref[...])
    comb = comb_ref[...]
    lane = lax.broadcasted_iota(jnp.int32, comb.shape, 1)
    cw = jnp.sum(jnp.where(lane == e, comb, 0.0), axis=-1, keepdims=True)
    act = (a * _sigmoid(a)) * u * cw
    acc_scr[...] += _mm(act.astype(BF16), wd_ref[...])

    @pl.when(e == pl.num_programs(1) - 1)
    def _():
        y = acc_scr[...]
        if final:
            y = _rmsnorm_rows(y, gf_ref[...])
        y_ref[...] = y


def _moe(final, x2, h2, comb, wg, wu, wd, gf, tm):
    t = x2.shape[0]
    return pl.pallas_call(
        functools.partial(_moe_kernel, final),
        grid=(t // tm, N_EXPERTS),
        in_specs=[
            pl.BlockSpec((tm, D_MODEL), lambda i, e: (i, 0)),
            pl.BlockSpec((tm, D_MODEL), lambda i, e: (i, 0)),
            pl.BlockSpec((tm, LANES), lambda i, e: (i, 0)),
            pl.BlockSpec((None, D_MODEL, EXPERT_FF), lambda i, e: (e, 0, 0)),
            pl.BlockSpec((None, D_MODEL, EXPERT_FF), lambda i, e: (e, 0, 0)),
            pl.BlockSpec((None, EXPERT_FF, D_MODEL), lambda i, e: (e, 0, 0)),
            pl.BlockSpec(gf.shape, lambda i, e: (0, 0)),
        ],
        out_specs=pl.BlockSpec((tm, D_MODEL), lambda i, e: (i, 0)),
        out_shape=jax.ShapeDtypeStruct((t, D_MODEL), F32),
        scratch_shapes=[pltpu.VMEM((tm, D_MODEL), F32)],
        compiler_params=pltpu.CompilerParams(
            dimension_semantics=("arbitrary", "arbitrary"), vmem_limit_bytes=VMEM_LIMIT),
        name="moe",
    )(x2, h2, comb, wg, wu, wd, gf)


def _block_diag_state(s):
    n = s.shape[0]
    out = jnp.zeros((n, A_HEADS, HEAD_DIM, A_HEADS, HEAD_DIM), s.dtype)
    st = jnp.swapaxes(s, -1, -2)
    for h in range(A_HEADS):
        out = out.at[:, h, :, h, :].set(st[:, h])
    return out.reshape(n, A_WIDTH, A_WIDTH)


def _state_blocks(sbd):
    n = sbd.shape[0]
    s5 = sbd.reshape(n, A_HEADS, HEAD_DIM, A_HEADS, HEAD_DIM)
    return jnp.stack([s5[:, h, :, h, :] for h in range(A_HEADS)], axis=1)


def _pad_rows(a, nb, per, to):
    w = a.shape[-1]
    a3 = a.reshape(nb, per, w)
    return jnp.pad(a3, ((0, 0), (0, to - per), (0, 0))).reshape(nb * to, w)


def kernel(x_prompt, x_sample, cache_k, cache_v, cache_logf, state_hgrn, page_table, norm_mix_gain, w_in, hgrn_lb_logits, hgrn_out_gain, fox_f_bias, fox_out_gain, sgu_v_gain, sgu_w_s, sgu_b, sgu_out_gain, w_out, norm_ffn_gain, router_group_w, router_group_b, router_expert_w, router_expert_b, expert_w_gate, expert_w_up, expert_w_down, norm_final_gain):
    nb, seq, _ = x_prompt.shape
    ndb, dseq, _ = x_sample.shape
    depth = w_in.shape[0]
    tp = nb * seq
    ts = ndb * dseq
    tm_p = 512
    assert seq % ATT_TQ == 0 and ATT_TQ % ATT_TK == 0 and seq % tm_p == 0
    assert ts % SUBLANES == 0 and ts <= CHUNK and dseq <= SUBLANES

    cm_np, mk_np = _hgrn_consts()
    cm = jnp.asarray(cm_np, BF16)
    mk = jnp.asarray(mk_np, F32)
    e256 = jnp.asarray(_block_ones(A_WIDTH, HEAD_DIM), BF16)
    place_np, qone_np, vone_np = _aug_consts()
    place = jnp.asarray(place_np, BF16)
    wms_np = np.zeros((2 * LANES, 2 * LANES), np.float32)
    sh_np = np.zeros((LANES, LANES), np.float32)
    for blk in range(2):
        wms_np[blk * LANES:blk * LANES + HEAD_DIM + 1, blk * LANES:(blk + 1) * LANES] = 1.0
    sh_np[np.arange(HEAD_DIM), HEAD_DIM + np.arange(HEAD_DIM)] = 1.0
    wms = jnp.asarray(wms_np, BF16)
    sh = jnp.asarray(sh_np, BF16)
    qone = jnp.asarray(qone_np)
    vone = jnp.asarray(vone_np)
    ar = np.arange(tm_p)
    tril_tm = jnp.asarray((ar[None, :] <= ar[:, None]).astype(np.float32), BF16)
    ac = np.arange(CHUNK)
    triu_pg = jnp.asarray((ac[:, None] <= ac[None, :]).astype(np.float32), BF16)
    rows = B_HEADS * SUBLANES
    rr = np.arange(rows)
    col = np.arange(B_WIDTH)
    bm = jnp.asarray((rr[:, None] // SUBLANES == col[None, :] // HEAD_DIM).astype(np.float32))
    tq_of_row = rr[:, None] % SUBLANES
    nm = jnp.asarray(((ac[None, :] < dseq) & (ac[None, :] <= tq_of_row)
                      & (tq_of_row < dseq)).astype(np.float32))

    sr = np.arange(ts)
    same_seq = jnp.asarray(((sr[:, None] // dseq) == (sr[None, :] // dseq))
                           & ((sr[None, :] % dseq) <= (sr[:, None] % dseq)))
    tril = jnp.asarray(np.tril(np.ones((CHUNK, CHUNK), np.float32)))

    ck = jnp.transpose(cache_k, (0, 1, 3, 4, 2))
    cv = jnp.transpose(cache_v, (0, 1, 3, 4, 2))
    clf = jnp.transpose(cache_logf, (0, 1, 3, 2))

    xp = x_prompt.reshape(tp, D_MODEL)
    xs = x_sample.reshape(ts, D_MODEL)
    zeros_state = jnp.zeros((nb, A_WIDTH, A_WIDTH), F32)
    n_levels_s = sum(1 for h in HGRN_LEVELS if h < dseq)

    outs_p = {k: [] for k in ("lf", "s")}
    kv_p = None
    outs_s = {k: [] for k in ("k", "v", "lf", "s", "vn")}

    for l in range(depth):
        wl = w_in[l]
        w_r32 = jnp.concatenate(
            [wl[:, 0:2560], wl[:, 2568:3080], wl[:, 2560:2568],
             jnp.zeros((D_MODEL, IN_COLS_PAD - 3080), F32)], axis=1)
        w_r = w_r32.astype(BF16)
        fb = jnp.pad(fox_f_bias[l][None, :], ((0, 0), (0, LANES - B_HEADS)))
        g_mix = norm_mix_gain[l][None, :]
        vg = sgu_v_gain[l][None, :]
        sog = sgu_out_gain[l][None, :]
        smat_p = (sgu_w_s[l] * tril).astype(BF16)
        sbias_p = jnp.repeat(sgu_b[l].T, HEAD_DIM, axis=1)
        w4 = sgu_w_s[l][:, :dseq, :dseq]
        smat_s = jnp.where(same_seq, jnp.tile(w4, (1, ndb, ndb)), 0.0)
        sbias_s = jnp.repeat(jnp.tile(sgu_b[l][:, :dseq].T, (ndb, 1)), HEAD_DIM, axis=1)
        hg = hgrn_out_gain[l][None, :]
        fg = fox_out_gain[l][None, :]
        fg_blk = jnp.pad(fox_out_gain[l].reshape(B_HEADS, HEAD_DIM),
                         ((0, 0), (0, LANES - HEAD_DIM))).reshape(1, AUG_WIDTH)
        wo32 = w_out[l]
        wo = wo32.astype(BF16)
        g_ffn = norm_ffn_gain[l][None, :]
        wrt = jnp.concatenate(
            [router_group_w[l].T,
             jnp.transpose(router_expert_w[l], (0, 2, 1)).reshape(N_EXPERTS, D_MODEL),
             jnp.zeros((32 - N_GROUPS - N_EXPERTS, D_MODEL), F32)], axis=0)
        brc = jnp.concatenate(
            [router_group_b[l], router_expert_b[l].reshape(-1),
             jnp.zeros((32 - N_GROUPS - N_EXPERTS,), F32)])[:, None]
        wg = expert_w_gate[l].reshape(N_EXPERTS, D_MODEL, EXPERT_FF).astype(BF16)
        wu = expert_w_up[l].reshape(N_EXPERTS, D_MODEL, EXPERT_FF).astype(BF16)
        wd = expert_w_down[l].reshape(N_EXPERTS, EXPERT_FF, D_MODEL).astype(BF16)
        gf = norm_final_gain[None, :]
        final = l == depth - 1

        consts_p = (g_mix, w_r, hgrn_lb_logits, fb, vg, smat_p, sbias_p, sog, e256,
                    tril_tm, place, qone, vone)
        (qa, lfa, ka, ia, ga, _, kb, vb, lfb, oc, _, qaug, kaug, vaug) = _inproj(
            l, True, seq // tm_p, False, xp, consts_p, tm_p, depth, kv_p)
        kv_p = (kb, vb)
        oa, st = _hgrn(False, len(HGRN_LEVELS), qa, lfa, ka, ia, ga, zeros_state, cm, mk, e256, hg, nb, seq // CHUNK)
        ob = _fox(qaug, kaug, vaug, fg_blk, wms, sh, nb, seq)
        x2, h2, comb = _outproj(False, xp, oa, ob, oc, wo, g_ffn, wrt, brc, tm_p)
        xp = _moe(final, x2, h2, comb, wg, wu, wd, gf, 1024 if tp % 1024 == 0 else tm_p)
        outs_p["lf"].append(lfb.reshape(nb, seq, B_HEADS))
        outs_p["s"].append(_state_blocks(st))

        consts_s = (g_mix, w_r32, hgrn_lb_logits, fb, vg, smat_s, sbias_s, sog, e256,
                    tril_tm, place, qone, vone)
        (qa, lfa, ka, ia, ga, qb, kb, vb, lfb, oc, vn, _, _, _) = _inproj(
            l, False, 1, True, xs, consts_s, ts, 1, None)
        pads = [_pad_rows(a, ndb, dseq, CHUNK) for a in (qa, lfa, ka, ia, ga)]
        oa_pad, st = _hgrn(True, n_levels_s, *pads, _block_diag_state(state_hgrn[l]), cm, mk, e256, hg, ndb, 1)
        oa = oa_pad.reshape(ndb, CHUNK, A_WIDTH)[:, :dseq].reshape(ts, A_WIDTH)
        q4 = jnp.pad(qb.reshape(ndb, dseq, B_HEADS, HEAD_DIM),
                     ((0, 0), (0, SUBLANES - dseq), (0, 0), (0, 0)))
        q_bd = (jnp.transpose(q4, (0, 2, 1, 3))[:, :, :, None, :]
                * jnp.eye(B_HEADS, dtype=F32)[None, :, None, :, None]).reshape(ndb, rows, B_WIDTH)
        k5 = kb.reshape(ndb, dseq, B_HEADS, HEAD_DIM)
        v5 = vb.reshape(ndb, dseq, B_HEADS, HEAD_DIM)
        lf3 = lfb.reshape(ndb, dseq, B_HEADS)
        padn = ((0, 0), (0, 0), (0, 0), (0, CHUNK - dseq))
        ob8 = _paged(l, page_table, q_bd, ck, cv, clf,
                     jnp.pad(jnp.transpose(k5, (0, 2, 3, 1)), padn),
                     jnp.pad(jnp.transpose(v5, (0, 2, 3, 1)), padn),
                     jnp.pad(jnp.transpose(lf3, (0, 2, 1)), padn[1:]),
                     triu_pg, bm, nm, fg)
        ob = ob8[:, :dseq].reshape(ts, B_WIDTH)
        x2, h2, comb = _outproj(True, xs, oa, ob, oc, wo32, g_ffn, wrt, brc, ts)
        xs = _moe(final, x2, h2, comb, wg, wu, wd, gf, ts)
        outs_s["k"].append(k5)
        outs_s["v"].append(v5)
        outs_s["lf"].append(lf3)
        outs_s["s"].append(_state_blocks(st))
        outs_s["vn"].append(vn.reshape(ndb, dseq, C_WIDTH))

    return (xp.reshape(nb, seq, D_MODEL), xs.reshape(ndb, dseq, D_MODEL),
            jnp.transpose(kv_p[0].reshape(depth, nb, B_HEADS, HEAD_DIM, seq), (0, 1, 4, 2, 3)),
            jnp.transpose(kv_p[1].reshape(depth, nb, B_HEADS, HEAD_DIM, seq), (0, 1, 4, 2, 3)),
            jnp.stack(outs_p["lf"]),
            jnp.stack(outs_p["s"]),
            jnp.stack(outs_s["k"]), jnp.stack(outs_s["v"]), jnp.stack(outs_s["lf"]),
            jnp.stack(outs_s["s"]), jnp.stack(outs_s["vn"]))
```

```python
import functools

import numpy as np
import jax
import jax.numpy as jnp
from jax import lax
from jax.experimental import pallas as pl
from jax.experimental.pallas import tpu as pltpu

F32 = jnp.float32
BF16 = jnp.bfloat16

D_MODEL = 1024
HEAD_DIM = 64
A_HEADS = 4
A_WIDTH = A_HEADS * HEAD_DIM
B_HEADS = 8
B_WIDTH = B_HEADS * HEAD_DIM
C_GROUPS = 4
C_WIDTH = C_GROUPS * HEAD_DIM
CHUNK = 128
N_GROUPS = 4
N_EXP = 4
N_EXPERTS = N_GROUPS * N_EXP
EXPERT_FF = D_MODEL // 4
RMS_EPS = 1e-6
NEG_INF = -1e30
LOG2E = 1.4426950408889634
LANES = 128
SUBLANES = 8
IN_COLS_PAD = 3200
AUG_WIDTH = B_HEADS * LANES
HGRN_LEVELS = (1, 2, 4, 8, 16, 32, 64)
ATT_TQ = 512
ATT_TK = 256
PAGES_PER_STEP = 16
VMEM_LIMIT = 48 * 1024 * 1024


class _Mx:
    def __init__(self, precise):
        self.precise = precise
        self.prec = lax.Precision.HIGHEST if precise else None

    def cast(self, x):
        return x.astype(F32) if self.precise else x.astype(BF16)

    def mm(self, a, b):
        return jnp.dot(self.cast(a), self.cast(b), preferred_element_type=F32, precision=self.prec)

    def mm_nt(self, a, b):
        return lax.dot_general(self.cast(a), self.cast(b), (((1,), (1,)), ((), ())),
                               preferred_element_type=F32, precision=self.prec)


def _mm(a, b):
    return jnp.dot(a, b, preferred_element_type=F32)


def _mm_nt(a, b):
    return lax.dot_general(a, b, (((1,), (1,)), ((), ())), preferred_element_type=F32)


def _split3(x):
    p1 = x.astype(BF16)
    r1 = x - p1.astype(F32)
    p2 = r1.astype(BF16)
    r2 = r1 - p2.astype(F32)
    return p1, p2, r2.astype(BF16)


def _split2(x):
    hi = x.astype(BF16)
    return hi, (x - hi.astype(F32)).astype(BF16)


def _dot01_left(m01, x):
    p1, p2, p3 = _split3(x)
    return _mm(m01, p1) + _mm(m01, p2) + _mm(m01, p3)


def _dot01_right(x, m01):
    p1, p2, p3 = _split3(x)
    return _mm(p1, m01) + _mm(p2, m01) + _mm(p3, m01)


def _sigmoid(x):
    return 1.0 / (1.0 + jnp.exp(-x))


def _log_sigmoid(x):
    return jnp.minimum(x, 0.0) - jnp.log1p(jnp.exp(-jnp.abs(x)))


def _gelu_tanh(x):
    return x * (0.5 * (1.0 + jnp.tanh(0.7978845608028654 * (x + 0.044715 * (x * x * x)))))


def _rmsnorm_rows(x, g):
    return x * lax.rsqrt(jnp.mean(x * x, axis=-1, keepdims=True) + RMS_EPS) * g


def _group_rmsnorm(x, e01, g):
    ms = _dot01_right(x * x, e01) * (1.0 / HEAD_DIM)
    return x * lax.rsqrt(ms + RMS_EPS) * g


def _head_blocks(x):
    rows = x.shape[0]
    left = lax.broadcasted_iota(jnp.int32, (rows, LANES), 1) < HEAD_DIM
    out = []
    for pr in range(B_HEADS // 2):
        xp = x[:, pr * LANES:(pr + 1) * LANES]
        out.append(jnp.where(left, xp, 0.0))
        out.append(jnp.where(left, pltpu.roll(xp, HEAD_DIM, 1), 0.0))
    return jnp.concatenate(out, axis=1)


def _block_ones(n, blk):
    i = np.arange(n)
    return (i[:, None] // blk == i[None, :] // blk).astype(np.float32)


def _hgrn_consts():
    n = CHUNK
    t = np.arange(n)[:, None]
    j = np.arange(n)[None, :]
    mats = [j <= t, j > t]
    masks = []
    for h in HGRN_LEVELS:
        same = (t // (2 * h)) == (j // (2 * h))
        off_j = j % (2 * h)
        off_t = t % (2 * h)
        mats.append((same & (off_j >= h) & (j <= t)) | (same & (off_j < h) & (j > t)))
        masks.append(same & (off_t >= h) & (off_j < h))
    cm = np.concatenate([m.astype(np.float32) for m in mats], axis=0)
    mk = np.stack([np.concatenate([m, m], axis=0).astype(np.float32) for m in masks])
    return cm, mk


def _aug_consts():
    place = np.zeros((3 * LANES, AUG_WIDTH), np.float32)
    qone = np.zeros((1, AUG_WIDTH), np.float32)
    vone = np.zeros((1, AUG_WIDTH), np.float32)
    for h in range(B_HEADS):
        for part in range(3):
            place[part * LANES + h, h * LANES + HEAD_DIM + part] = 1.0
            qone[0, h * LANES + HEAD_DIM + part] = 1.0
        vone[0, h * LANES + HEAD_DIM:(h + 1) * LANES] = 1.0
    return place, qone, vone


def _inproj_kernel(layer, prompt, tiles_per_seq, precise, n_aliased,
                   x_ref, g_ref, w_ref, lbl_ref, fb_ref, vg_ref, smat_ref, sbias_ref, sog_ref,
                   e_ref, tril_ref, place_ref, qone_ref, vone_ref, *rest):
    (qa_ref, lfa_ref, ka_ref, ia_ref, ga_ref, qb_ref, kb_ref, vb_ref, lfb_ref,
     oc_ref, vn_ref, qaug_ref, kaug_ref, vaug_ref, carry_scr) = rest[n_aliased:]
    mx = _Mx(precise)
    tm = x_ref.shape[0]
    h = _rmsnorm_rows(x_ref[...], g_ref[...])
    y = mx.mm(h, w_ref[...])
    a_q = y[:, 0:256]
    z = y[:, 256:512]
    a_i = y[:, 512:768]
    a_g = y[:, 768:1024]
    b_q = y[:, 1024:1536] * (HEAD_DIM ** -0.5)
    b_k = y[:, 1536:2048]
    b_v = y[:, 2048:2560]
    c_u = y[:, 2560:2816]
    c_v = y[:, 2816:3072]
    b_f = y[:, 3072:3200]

    lbl = lbl_ref[...]
    ex = jnp.exp(lbl - jnp.max(lbl, axis=0, keepdims=True))
    p = ex / jnp.sum(ex, axis=0, keepdims=True)
    cs = p[0:1]
    for j in range(1, layer + 1):
        cs = cs + p[j:j + 1]
    lb = cs - p[0:1]
    la = jnp.log(lb)
    lbb = jnp.log1p(-lb) + _log_sigmoid(z)
    lfa_ref[...] = jnp.maximum(la, lbb) + jnp.log1p(jnp.exp(-jnp.abs(la - lbb)))
    ka_ref[...] = (1.0 - lb) * _sigmoid(-z)
    qa_ref[...] = a_q * _sigmoid(a_q)
    ia_ref[...] = a_i
    ga_ref[...] = a_g * _sigmoid(a_g)

    qb_ref[...] = b_q
    if prompt:
        kb_ref[...] = b_k.T
        vb_ref[...] = b_v.T
    else:
        kb_ref[...] = b_k
        vb_ref[...] = b_v
    lfb = _log_sigmoid(b_f + fb_ref[...])
    lfb_ref[...] = lfb[:, 0:B_HEADS]

    if prompt:
        i = pl.program_id(0)

        @pl.when(i % tiles_per_seq == 0)
        def _():
            carry_scr[...] = jnp.zeros(carry_scr.shape, F32)

        c = carry_scr[0:1, :] + _dot01_left(tril_ref[...], lfb)
        carry_scr[...] = jnp.broadcast_to(c[tm - 1:tm, :], carry_scr.shape)
        n1, n2, n3 = _split3(c * (-LOG2E))
        bias = _mm(jnp.concatenate([n1, n2, n3], axis=1), place_ref[...])
        qaug_ref[...] = (_head_blocks(b_q * LOG2E) + qone_ref[...]).astype(BF16)
        kaug_ref[...] = (_head_blocks(b_k) + bias).astype(BF16)
        vaug_ref[...] = (_head_blocks(b_v) + vone_ref[...]).astype(BF16)
    else:
        qaug_ref[...] = jnp.zeros(qaug_ref.shape, BF16)
        kaug_ref[...] = jnp.zeros(kaug_ref.shape, BF16)
        vaug_ref[...] = jnp.zeros(vaug_ref.shape, BF16)

    e256 = e_ref[...]
    u = _gelu_tanh(c_u)
    vn = _group_rmsnorm(_gelu_tanh(c_v), e256, vg_ref[...])
    vn_ref[...] = vn
    left = lax.broadcasted_iota(jnp.int32, (CHUNK, LANES), 1) < HEAD_DIM
    zrows = []
    for c in range(tm // CHUNK):
        zp = []
        for pr in range(2):
            vp = vn[c * CHUNK:(c + 1) * CHUNK, pr * LANES:(pr + 1) * LANES]
            zp.append(jnp.where(left, mx.mm(smat_ref[2 * pr], vp), mx.mm(smat_ref[2 * pr + 1], vp)))
        zrows.append(jnp.concatenate(zp, axis=1) + sbias_ref[...])
    zc = zrows[0] if len(zrows) == 1 else jnp.concatenate(zrows, axis=0)
    oc_ref[...] = _group_rmsnorm(u * zc, e256, sog_ref[...]).astype(oc_ref.dtype)


KV_OUT = (6, 7)


def _inproj(layer, prompt, tiles_per_seq, precise, x2d, consts, tm, kv_depth, kv_prev):
    t = x2d.shape[0]
    row = lambda w: pl.BlockSpec((tm, w), lambda i: (i, 0))
    full = lambda a: pl.BlockSpec(a.shape, lambda i: (0,) * a.ndim)
    aug_rows = tm if prompt else SUBLANES
    aug_t = t if prompt else SUBLANES
    aug_spec = pl.BlockSpec((aug_rows, AUG_WIDTH), (lambda i: (i, 0)) if prompt else (lambda i: (0, 0)))
    outs = [(A_WIDTH, F32)] * 5 + [(B_WIDTH, F32)] * 3 + [(B_HEADS, F32), (C_WIDTH, F32 if precise else BF16), (C_WIDTH, F32)]
    out_shape = [jax.ShapeDtypeStruct((t, w), d) for w, d in outs]
    out_specs = [row(w) for w, _ in outs]
    for o in KV_OUT:
        if prompt:
            seq = tiles_per_seq * tm
            out_shape[o] = jax.ShapeDtypeStruct((kv_depth, t // seq, B_WIDTH, seq), F32)
            out_specs[o] = pl.BlockSpec(
                (None, None, B_WIDTH, tm),
                lambda i: (layer, i // tiles_per_seq, 0, i % tiles_per_seq))
        else:
            out_shape[o] = jax.ShapeDtypeStruct((t, B_WIDTH), F32)
    out_shape += [jax.ShapeDtypeStruct((aug_t, AUG_WIDTH), BF16)] * 3
    out_specs += [aug_spec] * 3
    inputs = [x2d, *consts]
    in_specs = [row(D_MODEL)] + [full(a) for a in consts]
    aliases = {}
    if kv_prev is not None:
        for o, buf in zip(KV_OUT, kv_prev):
            aliases[len(inputs)] = o
            inputs.append(buf)
            in_specs.append(pl.BlockSpec(memory_space=pl.ANY))
    return pl.pallas_call(
        functools.partial(_inproj_kernel, layer, prompt, tiles_per_seq, precise, len(aliases)),
        grid=(t // tm,),
        in_specs=in_specs,
        out_specs=out_specs,
        out_shape=out_shape,
        input_output_aliases=aliases,
        scratch_shapes=[pltpu.VMEM((SUBLANES, LANES), F32)],
        compiler_params=pltpu.CompilerParams(
            dimension_semantics=("arbitrary",), vmem_limit_bytes=VMEM_LIMIT),
        name="inproj",
    )(*inputs)


HGRN_SEQS_PER_STEP = 4


def _hgrn_kernel(precise, n_levels, q_ref, lf_ref, k_ref, i_ref, g_ref, st0_ref, cm_ref, mk_ref, e_ref,
                 gain_ref, o_ref, stout_ref, st_scr):
    c = pl.program_id(1)

    @pl.when(c == 0)
    def _():
        st_scr[...] = st0_ref[...]

    st_new = [_hgrn_chunk(precise, n_levels, q_ref.at[u], lf_ref.at[u], k_ref.at[u], i_ref.at[u],
                          g_ref.at[u], cm_ref, mk_ref, e_ref, gain_ref, o_ref.at[u], st_scr.at[u])
              for u in range(q_ref.shape[0])]

    @pl.when(c == pl.num_programs(1) - 1)
    def _():
        for u in range(q_ref.shape[0]):
            stout_ref[u] = st_new[u].T


def _hgrn_chunk(precise, n_levels, q_ref, lf_ref, k_ref, i_ref, g_ref, cm_ref, mk_ref, e_ref,
                gain_ref, o_ref, st_scr):
    mx = _Mx(precise)

    cmat = cm_ref[0:(2 + n_levels) * CHUNK, :]
    if precise:
        p1, p2, p3 = _split3(lf_ref[...])
        r = _mm(cmat, jnp.concatenate([p1, p2, p3], axis=1))
        r = r[:, 0:256] + r[:, 256:512] + r[:, 512:768]
    else:
        p1, p2 = _split2(lf_ref[...])
        r = _mm(cmat, jnp.concatenate([p1, p2], axis=1))
        r = r[:, 0:256] + r[:, 256:512]
    q = q_ref[...]
    k = k_ref[...]
    iv = i_ref[...]
    b = r[0:CHUNK]
    su = r[CHUNK:2 * CHUNK]
    qd = q * jnp.exp(b)
    kd = k * jnp.exp(su)

    left = lax.broadcasted_iota(jnp.int32, (CHUNK, LANES), 1) < HEAD_DIM
    amat = [None] * 2
    for lv in range(n_levels):
        ex = jnp.exp(r[(lv + 2) * CHUNK:(lv + 3) * CHUNK])
        ql = q * ex
        kl = k * ex
        m2 = mk_ref[lv]
        for pr in range(2):
            qp = ql[:, pr * LANES:(pr + 1) * LANES]
            kp = kl[:, pr * LANES:(pr + 1) * LANES]
            q2 = jnp.concatenate([jnp.where(left, qp, 0.0), jnp.where(left, 0.0, qp)], axis=0)
            s = m2 * mx.mm_nt(q2, kp)
            amat[pr] = s if amat[pr] is None else amat[pr] + s

    e256 = e_ref[...]
    o_pairs = []
    for pr in range(2):
        o2 = mx.mm(amat[pr], iv[:, pr * LANES:(pr + 1) * LANES])
        o_pairs.append(jnp.where(left, o2[0:CHUNK], o2[CHUNK:2 * CHUNK]))
    o_intra = jnp.concatenate(o_pairs, axis=1)
    o_diag = _dot01_right(q * k, e256) * iv
    st = st_scr[...]
    o = mx.mm_nt(qd, st) + o_intra + o_diag
    o_ref[...] = (_group_rmsnorm(o, e256, gain_ref[...]) * g_ref[...]).astype(o_ref.dtype)

    upd = mx.mm(iv.T, kd)
    st_new = st * jnp.exp(b[CHUNK - 1:CHUNK, :]) + e256.astype(F32) * upd
    st_scr[...] = st_new
    return st_new


def _hgrn(precise, n_levels, qa, lfa, ka, ia, ga, st0, cm, mk, e256, gain, nb, nc):
    t = qa.shape[0]
    n = max(d for d in range(1, HGRN_SEQS_PER_STEP + 1) if nb % d == 0)
    row = pl.BlockSpec((n, None, CHUNK, A_WIDTH), lambda b, c: (b, c, 0, 0))
    full = lambda a: pl.BlockSpec(a.shape, lambda b, c: (0,) * a.ndim)
    st_spec = pl.BlockSpec((n, A_WIDTH, A_WIDTH), lambda b, c: (b, 0, 0))
    as4 = lambda a: a.reshape(nb, nc, CHUNK, A_WIDTH)
    o, st = pl.pallas_call(
        functools.partial(_hgrn_kernel, precise, n_levels),
        grid=(nb // n, nc),
        in_specs=[row] * 5 + [st_spec, full(cm), full(mk), full(e256), full(gain)],
        out_specs=[row, st_spec],
        out_shape=[jax.ShapeDtypeStruct((nb, nc, CHUNK, A_WIDTH), F32 if precise else BF16),
                   jax.ShapeDtypeStruct((nb, A_WIDTH, A_WIDTH), F32)],
        scratch_shapes=[pltpu.VMEM((n, A_WIDTH, A_WIDTH), F32)],
        compiler_params=pltpu.CompilerParams(
            dimension_semantics=("arbitrary", "arbitrary"), vmem_limit_bytes=VMEM_LIMIT),
        name="hgrn",
    )(as4(qa), as4(lfa), as4(ka), as4(ia), as4(ga), st0, cm, mk, e256, gain)
    return o.reshape(t, A_WIDTH), st


def _fox_kernel(q_ref, k_ref, v_ref, gain_ref, wms_ref, sh_ref, o_ref, m_scr, acc_scr):
    i = pl.program_id(1)
    tq, tk = ATT_TQ, ATT_TK
    m_scr[...] = jnp.full(m_scr.shape, NEG_INF, F32)
    acc_scr[...] = jnp.zeros(acc_scr.shape, F32)
    n_full = (i * tq) // tk

    def block(r0, nr, ks, mask):
        rows = slice(r0, r0 + nr)
        for hd in range(B_HEADS):
            lanes = slice(hd * LANES, (hd + 1) * LANES)
            s = _mm_nt(q_ref[rows, lanes], k_ref[pl.ds(ks, tk), lanes])
            if mask is not None:
                s = jnp.where(mask, s, NEG_INF)
            m_prev = m_scr[hd, rows, :]
            m_new = jnp.maximum(m_prev, jnp.max(s, axis=-1, keepdims=True))
            alpha = jnp.exp2(m_prev - m_new)
            p = jnp.exp2(s - jnp.concatenate([m_new] * (tk // LANES), axis=1))
            acc_scr[hd, rows, :] = (alpha * acc_scr[hd, rows, :]
                                    + _mm(p.astype(BF16), v_ref[pl.ds(ks, tk), lanes]))
            m_scr[hd, rows, :] = m_new

    def body(j, carry):
        block(0, tq, pl.multiple_of(j * tk, tk), None)
        return carry

    lax.fori_loop(0, n_full, body, 0)
    tri = (lax.broadcasted_iota(jnp.int32, (tk, tk), 1)
           <= lax.broadcasted_iota(jnp.int32, (tk, tk), 0))
    for d in range(tq // tk):
        ks = pl.multiple_of((n_full + d) * tk, tk)
        block(d * tk, tk, ks, tri)
        if (d + 1) * tk < tq:
            block((d + 1) * tk, tq - (d + 1) * tk, ks, None)

    left = lax.broadcasted_iota(jnp.int32, (tq, 2 * LANES), 1) % LANES < HEAD_DIM
    for pr in range(B_HEADS // 2):
        a2 = jnp.concatenate([acc_scr[2 * pr], acc_scr[2 * pr + 1]], axis=1)
        y = a2 * a2
        y = jnp.where(left, y, y * (HEAD_DIM * RMS_EPS))
        yh, yl = _split2(y)
        z = _mm(yh, wms_ref[...]) + _mm(yl, wms_ref[...])
        on = a2 * lax.rsqrt(z * (1.0 / HEAD_DIM)) * gain_ref[:, 2 * pr * LANES:(2 * pr + 2) * LANES]
        onb = on.astype(BF16)
        pair = onb[:, 0:LANES].astype(F32) + _mm(onb[:, LANES:2 * LANES], sh_ref[...])
        o_ref[:, pr * LANES:(pr + 1) * LANES] = pair.astype(BF16)


def _fox(qaug, kaug, vaug, gain, wms, sh, nb, seq):
    t = qaug.shape[0]
    nq = seq // ATT_TQ
    resident = lambda: pl.BlockSpec((seq, AUG_WIDTH), lambda b, i: (b, 0), pipeline_mode=pl.Buffered(1))
    return pl.pallas_call(
        _fox_kernel,
        grid=(nb, nq),
        in_specs=[
            pl.BlockSpec((ATT_TQ, AUG_WIDTH), lambda b, i: (b * nq + i, 0)),
            resident(), resident(),
            pl.BlockSpec(gain.shape, lambda b, i: (0, 0)),
            pl.BlockSpec(wms.shape, lambda b, i: (0, 0)),
            pl.BlockSpec(sh.shape, lambda b, i: (0, 0)),
        ],
        out_specs=pl.BlockSpec((ATT_TQ, B_WIDTH), lambda b, i: (b * nq + i, 0)),
        out_shape=jax.ShapeDtypeStruct((t, B_WIDTH), BF16),
        scratch_shapes=[
            pltpu.VMEM((B_HEADS, ATT_TQ, LANES), F32),
            pltpu.VMEM((B_HEADS, ATT_TQ, LANES), F32),
        ],
        compiler_params=pltpu.CompilerParams(
            dimension_semantics=("arbitrary", "arbitrary"), vmem_limit_bytes=VMEM_LIMIT),
        name="fox_prompt",
    )(qaug, kaug, vaug, gain, wms, sh)


def _paged_kernel(layer, g_pages, pt_ref, q_ref, ck_hbm, cv_hbm, clf_hbm, kn_ref, vn_ref, lfn_ref,
                  triu_ref, bm_ref, nm_ref, gain_ref, o_ref,
                  m_scr, l_scr, acc_scr, carry_scr, kbuf, vbuf, lfbuf, sem):
    b = pl.program_id(0)
    p = pl.program_id(1)
    nb = pl.num_programs(0)
    last = pl.num_programs(1) - 1
    rows = q_ref.shape[0]
    flat = B_HEADS * HEAD_DIM
    slot = (b * last + p) % 2

    def page_copies(bb, pp, sl):
        cps = []
        for g in range(g_pages):
            page = pt_ref[bb, pp * g_pages + g]
            cps.append(pltpu.make_async_copy(ck_hbm.at[layer, page], kbuf.at[sl, g], sem.at[sl]))
            cps.append(pltpu.make_async_copy(cv_hbm.at[layer, page], vbuf.at[sl, g], sem.at[sl]))
            cps.append(pltpu.make_async_copy(clf_hbm.at[layer, page], lfbuf.at[sl, g], sem.at[sl]))
        return cps

    @pl.when((b == 0) & (p == 0))
    def _():
        for cp in page_copies(0, 0, 0):
            cp.start()

    @pl.when(p + 1 < last)
    def _():
        for cp in page_copies(b, p + 1, 1 - slot):
            cp.start()

    @pl.when((p + 1 == last) & (b + 1 < nb))
    def _():
        for cp in page_copies(b + 1, 0, 1 - slot):
            cp.start()

    @pl.when(p == 0)
    def _():
        m_scr[...] = jnp.full(m_scr.shape, NEG_INF, F32)
        l_scr[...] = jnp.zeros(l_scr.shape, F32)
        acc_scr[...] = jnp.zeros(acc_scr.shape, F32)
        carry_scr[...] = jnp.zeros(carry_scr.shape, F32)

    qh, ql = _split2(q_ref[...])
    q2 = jnp.concatenate([qh, ql], axis=0)

    def step(pages, mask):
        carry = carry_scr[...]
        ss = []
        for k_ref, _, lf_ref in pages:
            kh, kl = _split2(k_ref[...].reshape(flat, CHUNK))
            lf = jnp.concatenate([lf_ref[...], jnp.zeros((SUBLANES, CHUNK), F32)], axis=0)
            c = carry + _dot01_right(lf, triu_ref[...])[0:B_HEADS]
            carry = jnp.broadcast_to(c[:, CHUNK - 1:CHUNK], carry.shape)
            sk = _mm(q2, kh)
            s = sk[0:rows] + sk[rows:2 * rows] + _mm(qh, kl)
            s = (s.reshape(B_HEADS, SUBLANES, CHUNK) - c[:, None, :]).reshape(rows, CHUNK)
            if mask is not None:
                s = jnp.where(mask > 0.0, s, NEG_INF)
            ss.append(s)
        carry_scr[...] = carry
        s_all = ss[0] if len(ss) == 1 else jnp.concatenate(ss, axis=1)
        m_prev = m_scr[...]
        m_new = jnp.maximum(m_prev, jnp.max(s_all, axis=-1, keepdims=True))
        alpha = jnp.exp(m_prev - m_new)
        p_all = jnp.exp(s_all - jnp.concatenate([m_new] * len(ss), axis=1))
        l_scr[...] = alpha * l_scr[...] + jnp.sum(p_all, axis=-1, keepdims=True)
        m_scr[...] = m_new
        pv = None
        for g, (_, v_ref, _) in enumerate(pages):
            vh, vl = _split2(v_ref[...].reshape(flat, CHUNK))
            ph, plo = _split2(p_all[:, g * CHUNK:(g + 1) * CHUNK])
            pk = _mm_nt(jnp.concatenate([ph, plo], axis=0), vh)
            d = pk[0:rows] + pk[rows:2 * rows] + _mm_nt(ph, vl)
            pv = d if pv is None else pv + d
        acc_scr[...] = jnp.concatenate([alpha] * (flat // LANES), axis=1) * acc_scr[...] + pv

    @pl.when(p < last)
    def _():
        for cp in page_copies(b, p, slot):
            cp.wait()
        step([(kbuf.at[slot, g], vbuf.at[slot, g], lfbuf.at[slot, g]) for g in range(g_pages)], None)

    @pl.when(p == last)
    def _():
        step([(kn_ref, vn_ref, lfn_ref)], nm_ref[...])
        o = acc_scr[...] / jnp.concatenate([l_scr[...]] * (flat // LANES), axis=1)
        o = o * bm_ref[...]
        ms = jnp.sum(o * o, axis=-1, keepdims=True) * (1.0 / HEAD_DIM)
        on = o * lax.rsqrt(ms + RMS_EPS) * gain_ref[...]
        o_ref[...] = jnp.sum(on.reshape(B_HEADS, SUBLANES, flat), axis=0)


def _paged(layer, page_table, q_bd, ck, cv, clf, kn, vn, lfn, triu, bm, nm, gain):
    nb, n_pages = page_table.shape
    rows = q_bd.shape[1]
    g_pages = min(PAGES_PER_STEP, n_pages)
    assert n_pages % g_pages == 0
    n_steps = n_pages // g_pages

    kv_minor = (B_HEADS, HEAD_DIM, CHUNK)
    full2 = lambda a: pl.BlockSpec(a.shape, lambda b, p, pt: (0, 0))
    in_specs = [pl.BlockSpec((None, rows, B_WIDTH), lambda b, p, pt: (b, 0, 0))]
    in_specs += [pl.BlockSpec(memory_space=pl.ANY)] * 3
    in_specs += [
        pl.BlockSpec((None,) + kv_minor, lambda b, p, pt: (b, 0, 0, 0)),
        pl.BlockSpec((None,) + kv_minor, lambda b, p, pt: (b, 0, 0, 0)),
        pl.BlockSpec((None, B_HEADS, CHUNK), lambda b, p, pt: (b, 0, 0)),
        full2(triu), full2(bm), full2(nm), full2(gain),
    ]
    grid_spec = pltpu.PrefetchScalarGridSpec(
        num_scalar_prefetch=1,
        grid=(nb, n_steps + 1),
        in_specs=in_specs,
        out_specs=pl.BlockSpec((None, SUBLANES, B_WIDTH), lambda b, p, pt: (b, 0, 0)),
        scratch_shapes=[
            pltpu.VMEM((rows, LANES), F32),
            pltpu.VMEM((rows, LANES), F32),
            pltpu.VMEM((rows, B_WIDTH), F32),
            pltpu.VMEM((B_HEADS, CHUNK), F32),
            pltpu.VMEM((2, g_pages) + kv_minor, F32),
            pltpu.VMEM((2, g_pages) + kv_minor, F32),
            pltpu.VMEM((2, g_pages, B_HEADS, CHUNK), F32),
            pltpu.SemaphoreType.DMA((2,)),
        ],
    )
    return pl.pallas_call(
        functools.partial(_paged_kernel, layer, g_pages),
        grid_spec=grid_spec,
        out_shape=jax.ShapeDtypeStruct((nb, SUBLANES, B_WIDTH), F32),
        compiler_params=pltpu.CompilerParams(
            dimension_semantics=("arbitrary", "arbitrary"), vmem_limit_bytes=VMEM_LIMIT),
        name="fox_paged",
    )(page_table, q_bd, ck, cv, clf, kn, vn, lfn, triu, bm, nm, gain)


def _outproj_kernel(precise, x_ref, oa_ref, ob_ref, oc_ref, w_ref, g_ref, wrt_ref, brc_ref,
                    x2_ref, h2_ref, comb_ref):
    mx = _Mx(precise)
    tm = x_ref.shape[0]
    mix = (mx.mm(oa_ref[...], w_ref[0:A_WIDTH, :])
           + mx.mm(ob_ref[...], w_ref[A_WIDTH:A_WIDTH + B_WIDTH, :])
           + mx.mm(oc_ref[...], w_ref[A_WIDTH + B_WIDTH:, :]))
    x2 = x_ref[...] + mix
    x2_ref[...] = x2
    h2 = _rmsnorm_rows(x2, g_ref[...])
    h2b = h2.astype(BF16)
    h2_ref[...] = h2b
    h2l = (h2 - h2b.astype(F32)).astype(BF16)
    wr = wrt_ref[...]
    wrh = wr.astype(BF16)
    wrl = (wr - wrh.astype(F32)).astype(BF16)
    lg = _mm_nt(wrh, h2b) + _mm_nt(wrh, h2l) + _mm_nt(wrl, h2b) + brc_ref[...]
    row = [lg[i:i + 1, :] for i in range(N_GROUPS + N_EXPERTS)]

    g = row[0:N_GROUPS]
    gmax = jnp.maximum(jnp.maximum(g[0], g[1]), jnp.maximum(g[2], g[3]))
    gsel = jnp.where(g[0] == gmax, 0, jnp.where(g[1] == gmax, 1, jnp.where(g[2] == gmax, 2, 3)))
    den = (jnp.exp(g[0] - gmax) + jnp.exp(g[1] - gmax)) + (jnp.exp(g[2] - gmax) + jnp.exp(g[3] - gmax))
    gate = 1.0 / den
    le = []
    for e in range(N_EXP):
        le.append(jnp.where(gsel == 0, row[4 + e],
                            jnp.where(gsel == 1, row[8 + e],
                                      jnp.where(gsel == 2, row[12 + e], row[16 + e]))))
    v1 = jnp.maximum(jnp.maximum(le[0], le[1]), jnp.maximum(le[2], le[3]))
    i1 = jnp.where(le[0] == v1, 0, jnp.where(le[1] == v1, 1, jnp.where(le[2] == v1, 2, 3)))
    le2 = [jnp.where(i1 == e, -jnp.inf, le[e]) for e in range(N_EXP)]
    v2 = jnp.maximum(jnp.maximum(le2[0], le2[1]), jnp.maximum(le2[2], le2[3]))
    i2 = jnp.where(le2[0] == v2, 0, jnp.where(le2[1] == v2, 1, jnp.where(le2[2] == v2, 2, 3)))
    ex = jnp.exp(v2 - v1)
    w1 = 1.0 / (1.0 + ex)
    w2 = ex * w1
    rid = lax.broadcasted_iota(jnp.int32, (N_EXPERTS, tm), 0)
    combt = jnp.zeros((N_EXPERTS, tm), F32)
    for gi in range(N_GROUPS):
        for e in range(N_EXP):
            fine = jnp.where(i1 == e, w1, 0.0) + jnp.where(i2 == e, w2, 0.0)
            val = jnp.where(gsel == gi, gate * fine, 0.0)
            combt = jnp.where(rid == gi * N_EXP + e, val, combt)
    combt = jnp.concatenate([combt, jnp.zeros((LANES - N_EXPERTS, tm), F32)], axis=0)
    comb_ref[...] = combt.T


def _outproj(precise, x2d, oa, ob, oc, w_out, g, wrt, brc, tm):
    t = x2d.shape[0]
    row = lambda w: pl.BlockSpec((tm, w), lambda i: (i, 0))
    full = lambda a: pl.BlockSpec(a.shape, lambda i: (0,) * a.ndim)
    return pl.pallas_call(
        functools.partial(_outproj_kernel, precise),
        grid=(t // tm,),
        in_specs=[row(D_MODEL), row(A_WIDTH), row(B_WIDTH), row(C_WIDTH),
                  full(w_out), full(g), full(wrt), full(brc)],
        out_specs=[row(D_MODEL), row(D_MODEL), row(LANES)],
        out_shape=[jax.ShapeDtypeStruct((t, D_MODEL), F32),
                   jax.ShapeDtypeStruct((t, D_MODEL), BF16),
                   jax.ShapeDtypeStruct((t, LANES), F32)],
        compiler_params=pltpu.CompilerParams(
            dimension_semantics=("arbitrary",), vmem_limit_bytes=VMEM_LIMIT),
        name="outproj_router",
    )(x2d, oa, ob, oc, w_out, g, wrt, brc)


def _moe_kernel(final, x2_ref, h2_ref, comb_ref, wg_ref, wu_ref, wd_ref, gf_ref, y_ref, acc_scr):
    e = pl.program_id(1)

    @pl.when(e == 0)
    def _():
        acc_scr[...] = x2_ref[...]

    h = h2_ref[...]
    a = _mm(h, wg_ref[...])
    u = _mm(h, wu_ref[...])
    comb = comb_ref[...]
    lane = lax.broadcasted_iota(jnp.int32, comb.shape, 1)
    cw = jnp.sum(jnp.where(lane == e, comb, 0.0), axis=-1, keepdims=True)
    act = (a * _sigmoid(a)) * u * cw
    acc_scr[...] += _mm(act.astype(BF16), wd_ref[...])

    @pl.when(e == pl.num_programs(1) - 1)
    def _():
        y = acc_scr[...]
        if final:
            y = _rmsnorm_rows(y, gf_ref[...])
        y_ref[...] = y


def _moe(final, x2, h2, comb, wg, wu, wd, gf, tm):
    t = x2.shape[0]
    return pl.pallas_call(
        functools.partial(_moe_kernel, final),
        grid=(t // tm, N_EXPERTS),
        in_specs=[
            pl.BlockSpec((tm, D_MODEL), lambda i, e: (i, 0)),
            pl.BlockSpec((tm, D_MODEL), lambda i, e: (i, 0)),
            pl.BlockSpec((tm, LANES), lambda i, e: (i, 0)),
            pl.BlockSpec((None, D_MODEL, EXPERT_FF), lambda i, e: (e, 0, 0)),
            pl.BlockSpec((None, D_MODEL, EXPERT_FF), lambda i, e: (e, 0, 0)),
            pl.BlockSpec((None, EXPERT_FF, D_MODEL), lambda i, e: (e, 0, 0)),
            pl.BlockSpec(gf.shape, lambda i, e: (0, 0)),
        ],
        out_specs=pl.BlockSpec((tm, D_MODEL), lambda i, e: (i, 0)),
        out_shape=jax.ShapeDtypeStruct((t, D_MODEL), F32),
        scratch_shapes=[pltpu.VMEM((tm, D_MODEL), F32)],
        compiler_params=pltpu.CompilerParams(
            dimension_semantics=("arbitrary", "arbitrary"), vmem_limit_bytes=VMEM_LIMIT),
        name="moe",
    )(x2, h2, comb, wg, wu, wd, gf)


def _block_diag_state(s):
    n = s.shape[0]
    out = jnp.zeros((n, A_HEADS, HEAD_DIM, A_HEADS, HEAD_DIM), s.dtype)
    st = jnp.swapaxes(s, -1, -2)
    for h in range(A_HEADS):
        out = out.at[:, h, :, h, :].set(st[:, h])
    return out.reshape(n, A_WIDTH, A_WIDTH)


def _state_blocks(sbd):
    n = sbd.shape[0]
    s5 = sbd.reshape(n, A_HEADS, HEAD_DIM, A_HEADS, HEAD_DIM)
    return jnp.stack([s5[:, h, :, h, :] for h in range(A_HEADS)], axis=1)


def _pad_rows(a, nb, per, to):
    w = a.shape[-1]
    a3 = a.reshape(nb, per, w)
    return jnp.pad(a3, ((0, 0), (0, to - per), (0, 0))).reshape(nb * to, w)


def kernel(x_prompt, x_sample, cache_k, cache_v, cache_logf, state_hgrn, page_table, norm_mix_gain, w_in, hgrn_lb_logits, hgrn_out_gain, fox_f_bias, fox_out_gain, sgu_v_gain, sgu_w_s, sgu_b, sgu_out_gain, w_out, norm_ffn_gain, router_group_w, router_group_b, router_expert_w, router_expert_b, expert_w_gate, expert_w_up, expert_w_down, norm_final_gain):
    nb, seq, _ = x_prompt.shape
    ndb, dseq, _ = x_sample.shape
    depth = w_in.shape[0]
    tp = nb * seq
    ts = ndb * dseq
    tm_p = 512
    assert seq % ATT_TQ == 0 and ATT_TQ % ATT_TK == 0 and seq % tm_p == 0
    assert ts % SUBLANES == 0 and ts <= CHUNK and dseq <= SUBLANES

    cm_np, mk_np = _hgrn_consts()
    cm = jnp.asarray(cm_np, BF16)
    mk = jnp.asarray(mk_np, F32)
    e256 = jnp.asarray(_block_ones(A_WIDTH, HEAD_DIM), BF16)
    place_np, qone_np, vone_np = _aug_consts()
    place = jnp.asarray(place_np, BF16)
    wms_np = np.zeros((2 * LANES, 2 * LANES), np.float32)
    sh_np = np.zeros((LANES, LANES), np.float32)
    for blk in range(2):
        wms_np[blk * LANES:blk * LANES + HEAD_DIM + 1, blk * LANES:(blk + 1) * LANES] = 1.0
    sh_np[np.arange(HEAD_DIM), HEAD_DIM + np.arange(HEAD_DIM)] = 1.0
    wms = jnp.asarray(wms_np, BF16)
    sh = jnp.asarray(sh_np, BF16)
    qone = jnp.asarray(qone_np)
    vone = jnp.asarray(vone_np)
    ar = np.arange(tm_p)
    tril_tm = jnp.asarray((ar[None, :] <= ar[:, None]).astype(np.float32), BF16)
    ac = np.arange(CHUNK)
    triu_pg = jnp.asarray((ac[:, None] <= ac[None, :]).astype(np.float32), BF16)
    rows = B_HEADS * SUBLANES
    rr = np.arange(rows)
    col = np.arange(B_WIDTH)
    bm = jnp.asarray((rr[:, None] // SUBLANES == col[None, :] // HEAD_DIM).astype(np.float32))
    tq_of_row = rr[:, None] % SUBLANES
    nm = jnp.asarray(((ac[None, :] < dseq) & (ac[None, :] <= tq_of_row)
                      & (tq_of_row < dseq)).astype(np.float32))

    sr = np.arange(ts)
    same_seq = jnp.asarray(((sr[:, None] // dseq) == (sr[None, :] // dseq))
                           & ((sr[None, :] % dseq) <= (sr[:, None] % dseq)))
    tril = jnp.asarray(np.tril(np.ones((CHUNK, CHUNK), np.float32)))

    ck = jnp.transpose(cache_k, (0, 1, 3, 4, 2))
    cv = jnp.transpose(cache_v, (0, 1, 3, 4, 2))
    clf = jnp.transpose(cache_logf, (0, 1, 3, 2))

    xp = x_prompt.reshape(tp, D_MODEL)
    xs = x_sample.reshape(ts, D_MODEL)
    zeros_state = jnp.zeros((nb, A_WIDTH, A_WIDTH), F32)
    n_levels_s = sum(1 for h in HGRN_LEVELS if h < dseq)

    outs_p = {k: [] for k in ("lf", "s")}
    kv_p = None
    outs_s = {k: [] for k in ("k", "v", "lf", "s", "vn")}

    for l in range(depth):
        wl = w_in[l]
        w_r32 = jnp.concatenate(
            [wl[:, 0:2560], wl[:, 2568:3080], wl[:, 2560:2568],
             jnp.zeros((D_MODEL, IN_COLS_PAD - 3080), F32)], axis=1)
        w_r = w_r32.astype(BF16)
        fb = jnp.pad(fox_f_bias[l][None, :], ((0, 0), (0, LANES - B_HEADS)))
        g_mix = norm_mix_gain[l][None, :]
        vg = sgu_v_gain[l][None, :]
        sog = sgu_out_gain[l][None, :]
        smat_p = (sgu_w_s[l] * tril).astype(BF16)
        sbias_p = jnp.repeat(sgu_b[l].T, HEAD_DIM, axis=1)
        w4 = sgu_w_s[l][:, :dseq, :dseq]
        smat_s = jnp.where(same_seq, jnp.tile(w4, (1, ndb, ndb)), 0.0)
        sbias_s = jnp.repeat(jnp.tile(sgu_b[l][:, :dseq].T, (ndb, 1)), HEAD_DIM, axis=1)
        hg = hgrn_out_gain[l][None, :]
        fg = fox_out_gain[l][None, :]
        fg_blk = jnp.pad(fox_out_gain[l].reshape(B_HEADS, HEAD_DIM),
                         ((0, 0), (0, LANES - HEAD_DIM))).reshape(1, AUG_WIDTH)
        wo32 = w_out[l]
        wo = wo32.astype(BF16)
        g_ffn = norm_ffn_gain[l][None, :]
        wrt = jnp.concatenate(
            [router_group_w[l].T,
             jnp.transpose(router_expert_w[l], (0, 2, 1)).reshape(N_EXPERTS, D_MODEL),
             jnp.zeros((32 - N_GROUPS - N_EXPERTS, D_MODEL), F32)], axis=0)
        brc = jnp.concatenate(
            [router_group_b[l], router_expert_b[l].reshape(-1),
             jnp.zeros((32 - N_GROUPS - N_EXPERTS,), F32)])[:, None]
        wg = expert_w_gate[l].reshape(N_EXPERTS, D_MODEL, EXPERT_FF).astype(BF16)
        wu = expert_w_up[l].reshape(N_EXPERTS, D_MODEL, EXPERT_FF).astype(BF16)
        wd = expert_w_down[l].reshape(N_EXPERTS, EXPERT_FF, D_MODEL).astype(BF16)
        gf = norm_final_gain[None, :]
        final = l == depth - 1

        consts_p = (g_mix, w_r, hgrn_lb_logits, fb, vg, smat_p, sbias_p, sog, e256,
                    tril_tm, place, qone, vone)
        (qa, lfa, ka, ia, ga, _, kb, vb, lfb, oc, _, qaug, kaug, vaug) = _inproj(
            l, True, seq // tm_p, False, xp, consts_p, tm_p, depth, kv_p)
        kv_p = (kb, vb)
        oa, st = _hgrn(False, len(HGRN_LEVELS), qa, lfa, ka, ia, ga, zeros_state, cm, mk, e256, hg, nb, seq // CHUNK)
        ob = _fox(qaug, kaug, vaug, fg_blk, wms, sh, nb, seq)
        x2, h2, comb = _outproj(False, xp, oa, ob, oc, wo, g_ffn, wrt, brc, tm_p)
        xp = _moe(final, x2, h2, comb, wg, wu, wd, gf, 1024 if tp % 1024 == 0 else tm_p)
        outs_p["lf"].append(lfb.reshape(nb, seq, B_HEADS))
        outs_p["s"].append(_state_blocks(st))

        consts_s = (g_mix, w_r32, hgrn_lb_logits, fb, vg, smat_s, sbias_s, sog, e256,
                    tril_tm, place, qone, vone)
        (qa, lfa, ka, ia, ga, qb, kb, vb, lfb, oc, vn, _, _, _) = _inproj(
            l, False, 1, True, xs, consts_s, ts, 1, None)
        pads = [_pad_rows(a, ndb, dseq, CHUNK) for a in (qa, lfa, ka, ia, ga)]
        oa_pad, st = _hgrn(True, n_levels_s, *pads, _block_diag_state(state_hgrn[l]), cm, mk, e256, hg, ndb, 1)
        oa = oa_pad.reshape(ndb, CHUNK, A_WIDTH)[:, :dseq].reshape(ts, A_WIDTH)
        q4 = jnp.pad(qb.reshape(ndb, dseq, B_HEADS, HEAD_DIM),
                     ((0, 0), (0, SUBLANES - dseq), (0, 0), (0, 0)))
        q_bd = (jnp.transpose(q4, (0, 2, 1, 3))[:, :, :, None, :]
                * jnp.eye(B_HEADS, dtype=F32)[None, :, None, :, None]).reshape(ndb, rows, B_WIDTH)
        k5 = kb.reshape(ndb, dseq, B_HEADS, HEAD_DIM)
        v5 = vb.reshape(ndb, dseq, B_HEADS, HEAD_DIM)
        lf3 = lfb.reshape(ndb, dseq, B_HEADS)
        padn = ((0, 0), (0, 0), (0, 0), (0, CHUNK - dseq))
        ob8 = _paged(l, page_table, q_bd, ck, cv, clf,
                     jnp.pad(jnp.transpose(k5, (0, 2, 3, 1)), padn),
                     jnp.pad(jnp.transpose(v5, (0, 2, 3, 1)), padn),
                     jnp.pad(jnp.transpose(lf3, (0, 2, 1)), padn[1:]),
                     triu_pg, bm, nm, fg)
        ob = ob8[:, :dseq].reshape(ts, B_WIDTH)
        x2, h2, comb = _outproj(True, xs, oa, ob, oc, wo32, g_ffn, wrt, brc, ts)
        xs = _moe(final, x2, h2, comb, wg, wu, wd, gf, ts)
        outs_s["k"].append(k5)
        outs_s["v"].append(v5)
        outs_s["lf"].append(lf3)
        outs_s["s"].append(_state_blocks(st))
        outs_s["vn"].append(vn.reshape(ndb, dseq, C_WIDTH))

    return (xp.reshape(nb, seq, D_MODEL), xs.reshape(ndb, dseq, D_MODEL),
            jnp.transpose(kv_p[0].reshape(depth, nb, B_HEADS, HEAD_DIM, seq), (0, 1, 4, 2, 3)),
            jnp.transpose(kv_p[1].reshape(depth, nb, B_HEADS, HEAD_DIM, seq), (0, 1, 4, 2, 3)),
            jnp.stack(outs_p["lf"]),
            jnp.stack(outs_p["s"]),
            jnp.stack(outs_s["k"]), jnp.stack(outs_s["v"]), jnp.stack(outs_s["lf"]),
            jnp.stack(outs_s["s"]), jnp.stack(outs_s["vn"]))
```

```python
import functools

import numpy as np
import jax
import jax.numpy as jnp
from jax import lax
from jax.experimental import pallas as pl
from jax.experimental.pallas import tpu as pltpu

F32 = jnp.float32
BF16 = jnp.bfloat16

D_MODEL = 1024
HEAD_DIM = 64
A_HEADS = 4
A_WIDTH = A_HEADS * HEAD_DIM
B_HEADS = 8
B_WIDTH = B_HEADS * HEAD_DIM
C_GROUPS = 4
C_WIDTH = C_GROUPS * HEAD_DIM
CHUNK = 128
N_GROUPS = 4
N_EXP = 4
N_EXPERTS = N_GROUPS * N_EXP
EXPERT_FF = D_MODEL // 4
RMS_EPS = 1e-6
NEG_INF = -1e30
LOG2E = 1.4426950408889634
LANES = 128
SUBLANES = 8
IN_COLS_PAD = 3200
AUG_WIDTH = B_HEADS * LANES
HGRN_LEVELS = (1, 2, 4, 8, 16, 32, 64)
ATT_TQ = 512
ATT_TK = 256
PAGES_PER_STEP = 16
VMEM_LIMIT = 48 * 1024 * 1024


class _Mx:
    def __init__(self, precise):
        self.precise = precise
        self.prec = lax.Precision.HIGHEST if precise else None

    def cast(self, x):
        return x.astype(F32) if self.precise else x.astype(BF16)

    def mm(self, a, b):
        return jnp.dot(self.cast(a), self.cast(b), preferred_element_type=F32, precision=self.prec)

    def mm_nt(self, a, b):
        return lax.dot_general(self.cast(a), self.cast(b), (((1,), (1,)), ((), ())),
                               preferred_element_type=F32, precision=self.prec)


def _mm(a, b):
    return jnp.dot(a, b, preferred_element_type=F32)


def _mm_nt(a, b):
    return lax.dot_general(a, b, (((1,), (1,)), ((), ())), preferred_element_type=F32)


def _split3(x):
    p1 = x.astype(BF16)
    r1 = x - p1.astype(F32)
    p2 = r1.astype(BF16)
    r2 = r1 - p2.astype(F32)
    return p1, p2, r2.astype(BF16)


def _split2(x):
    hi = x.astype(BF16)
    return hi, (x - hi.astype(F32)).astype(BF16)


def _dot01_left(m01, x):
    p1, p2, p3 = _split3(x)
    return _mm(m01, p1) + _mm(m01, p2) + _mm(m01, p3)


def _dot01_right(x, m01):
    p1, p2, p3 = _split3(x)
    return _mm(p1, m01) + _mm(p2, m01) + _mm(p3, m01)


def _sigmoid(x):
    return 1.0 / (1.0 + jnp.exp(-x))


def _log_sigmoid(x):
    return jnp.minimum(x, 0.0) - jnp.log1p(jnp.exp(-jnp.abs(x)))


def _gelu_tanh(x):
    return x * (0.5 * (1.0 + jnp.tanh(0.7978845608028654 * (x + 0.044715 * (x * x * x)))))


def _rmsnorm_rows(x, g):
    return x * lax.rsqrt(jnp.mean(x * x, axis=-1, keepdims=True) + RMS_EPS) * g


def _group_rmsnorm(x, e01, g):
    ms = _dot01_right(x * x, e01) * (1.0 / HEAD_DIM)
    return x * lax.rsqrt(ms + RMS_EPS) * g


def _head_blocks(x):
    rows = x.shape[0]
    left = lax.broadcasted_iota(jnp.int32, (rows, LANES), 1) < HEAD_DIM
    out = []
    for pr in range(B_HEADS // 2):
        xp = x[:, pr * LANES:(pr + 1) * LANES]
        out.append(jnp.where(left, xp, 0.0))
        out.append(jnp.where(left, pltpu.roll(xp, HEAD_DIM, 1), 0.0))
    return jnp.concatenate(out, axis=1)


def _block_ones(n, blk):
    i = np.arange(n)
    return (i[:, None] // blk == i[None, :] // blk).astype(np.float32)


def _hgrn_consts():
    n = CHUNK
    t = np.arange(n)[:, None]
    j = np.arange(n)[None, :]
    mats = [j <= t, j > t]
    masks = []
    for h in HGRN_LEVELS:
        same = (t // (2 * h)) == (j // (2 * h))
        off_j = j % (2 * h)
        off_t = t % (2 * h)
        mats.append((same & (off_j >= h) & (j <= t)) | (same & (off_j < h) & (j > t)))
        masks.append(same & (off_t >= h) & (off_j < h))
    cm = np.concatenate([m.astype(np.float32) for m in mats], axis=0)
    mk = np.stack([np.concatenate([m, m], axis=0).astype(np.float32) for m in masks])
    return cm, mk


def _aug_consts():
    place = np.zeros((3 * LANES, AUG_WIDTH), np.float32)
    qone = np.zeros((1, AUG_WIDTH), np.float32)
    vone = np.zeros((1, AUG_WIDTH), np.float32)
    for h in range(B_HEADS):
        for part in range(3):
            place[part * LANES + h, h * LANES + HEAD_DIM + part] = 1.0
            qone[0, h * LANES + HEAD_DIM + part] = 1.0
        vone[0, h * LANES + HEAD_DIM:(h + 1) * LANES] = 1.0
    return place, qone, vone


def _inproj_kernel(layer, prompt, tiles_per_seq, precise, n_aliased,
                   x_ref, g_ref, w_ref, lbl_ref, fb_ref, vg_ref, smat_ref, sbias_ref, sog_ref,
                   e_ref, tril_ref, place_ref, qone_ref, vone_ref, *rest):
    (qa_ref, lfa_ref, ka_ref, ia_ref, ga_ref, qb_ref, kb_ref, vb_ref, lfb_ref,
     oc_ref, vn_ref, qaug_ref, kaug_ref, vaug_ref, carry_scr) = rest[n_aliased:]
    mx = _Mx(precise)
    tm = x_ref.shape[0]
    h = _rmsnorm_rows(x_ref[...], g_ref[...])
    y = mx.mm(h, w_ref[...])
    a_q = y[:, 0:256]
    z = y[:, 256:512]
    a_i = y[:, 512:768]
    a_g = y[:, 768:1024]
    b_q = y[:, 1024:1536] * (HEAD_DIM ** -0.5)
    b_k = y[:, 1536:2048]
    b_v = y[:, 2048:2560]
    c_u = y[:, 2560:2816]
    c_v = y[:, 2816:3072]
    b_f = y[:, 3072:3200]

    lbl = lbl_ref[...]
    ex = jnp.exp(lbl - jnp.max(lbl, axis=0, keepdims=True))
    p = ex / jnp.sum(ex, axis=0, keepdims=True)
    cs = p[0:1]
    for j in range(1, layer + 1):
        cs = cs + p[j:j + 1]
    lb = cs - p[0:1]
    la = jnp.log(lb)
    lbb = jnp.log1p(-lb) + _log_sigmoid(z)
    lfa_ref[...] = jnp.maximum(la, lbb) + jnp.log1p(jnp.exp(-jnp.abs(la - lbb)))
    ka_ref[...] = (1.0 - lb) * _sigmoid(-z)
    qa_ref[...] = a_q * _sigmoid(a_q)
    ia_ref[...] = a_i
    ga_ref[...] = a_g * _sigmoid(a_g)

    qb_ref[...] = b_q
    if prompt:
        kb_ref[...] = b_k.T
        vb_ref[...] = b_v.T
    else:
        kb_ref[...] = b_k
        vb_ref[...] = b_v
    lfb = _log_sigmoid(b_f + fb_ref[...])
    lfb_ref[...] = lfb[:, 0:B_HEADS]

    if prompt:
        i = pl.program_id(0)

        @pl.when(i % tiles_per_seq == 0)
        def _():
            carry_scr[...] = jnp.zeros(carry_scr.shape, F32)

        c = carry_scr[0:1, :] + _dot01_left(tril_ref[...], lfb)
        carry_scr[...] = jnp.broadcast_to(c[tm - 1:tm, :], carry_scr.shape)
        n1, n2, n3 = _split3(c * (-LOG2E))
        bias = _mm(jnp.concatenate([n1, n2, n3], axis=1), place_ref[...])
        qaug_ref[...] = (_head_blocks(b_q * LOG2E) + qone_ref[...]).astype(BF16)
        kaug_ref[...] = (_head_blocks(b_k) + bias).astype(BF16)
        vaug_ref[...] = (_head_blocks(b_v) + vone_ref[...]).astype(BF16)
    else:
        qaug_ref[...] = jnp.zeros(qaug_ref.shape, BF16)
        kaug_ref[...] = jnp.zeros(kaug_ref.shape, BF16)
        vaug_ref[...] = jnp.zeros(vaug_ref.shape, BF16)

    e256 = e_ref[...]
    u = _gelu_tanh(c_u)
    vn = _group_rmsnorm(_gelu_tanh(c_v), e256, vg_ref[...])
    vn_ref[...] = vn
    left = lax.broadcasted_iota(jnp.int32, (CHUNK, LANES), 1) < HEAD_DIM
    zrows = []
    for c in range(tm // CHUNK):
        zp = []
        for pr in range(2):
            vp = vn[c * CHUNK:(c + 1) * CHUNK, pr * LANES:(pr + 1) * LANES]
            zp.append(jnp.where(left, mx.mm(smat_ref[2 * pr], vp), mx.mm(smat_ref[2 * pr + 1], vp)))
        zrows.append(jnp.concatenate(zp, axis=1) + sbias_ref[...])
    zc = zrows[0] if len(zrows) == 1 else jnp.concatenate(zrows, axis=0)
    oc_ref[...] = _group_rmsnorm(u * zc, e256, sog_ref[...]).astype(oc_ref.dtype)


KV_OUT = (6, 7)


def _inproj(layer, prompt, tiles_per_seq, precise, x2d, consts, tm, kv_depth, kv_prev):
    t = x2d.shape[0]
    row = lambda w: pl.BlockSpec((tm, w), lambda i: (i, 0))
    full = lambda a: pl.BlockSpec(a.shape, lambda i: (0,) * a.ndim)
    aug_rows = tm if prompt else SUBLANES
    aug_t = t if prompt else SUBLANES
    aug_spec = pl.BlockSpec((aug_rows, AUG_WIDTH), (lambda i: (i, 0)) if prompt else (lambda i: (0, 0)))
    outs = [(A_WIDTH, F32)] * 5 + [(B_WIDTH, F32)] * 3 + [(B_HEADS, F32), (C_WIDTH, F32 if precise else BF16), (C_WIDTH, F32)]
    out_shape = [jax.ShapeDtypeStruct((t, w), d) for w, d in outs]
    out_specs = [row(w) for w, _ in outs]
    for o in KV_OUT:
        if prompt:
            seq = tiles_per_seq * tm
            out_shape[o] = jax.ShapeDtypeStruct((kv_depth, t // seq, B_WIDTH, seq), F32)
            out_specs[o] = pl.BlockSpec(
                (None, None, B_WIDTH, tm),
                lambda i: (layer, i // tiles_per_seq, 0, i % tiles_per_seq))
        else:
            out_shape[o] = jax.ShapeDtypeStruct((t, B_WIDTH), F32)
    out_shape += [jax.ShapeDtypeStruct((aug_t, AUG_WIDTH), BF16)] * 3
    out_specs += [aug_spec] * 3
    inputs = [x2d, *consts]
    in_specs = [row(D_MODEL)] + [full(a) for a in consts]
    aliases = {}
    if kv_prev is not None:
        for o, buf in zip(KV_OUT, kv_prev):
            aliases[len(inputs)] = o
            inputs.append(buf)
            in_specs.append(pl.BlockSpec(memory_space=pl.ANY))
    return pl.pallas_call(
        functools.partial(_inproj_kernel, layer, prompt, tiles_per_seq, precise, len(aliases)),
        grid=(t // tm,),
        in_specs=in_specs,
        out_specs=out_specs,
        out_shape=out_shape,
        input_output_aliases=aliases,
        scratch_shapes=[pltpu.VMEM((SUBLANES, LANES), F32)],
        compiler_params=pltpu.CompilerParams(
            dimension_semantics=("arbitrary",), vmem_limit_bytes=VMEM_LIMIT),
        name="inproj",
    )(*inputs)


HGRN_SEQS_PER_STEP = 4


def _hgrn_kernel(precise, n_levels, q_ref, lf_ref, k_ref, i_ref, g_ref, st0_ref, cm_ref, mk_ref, e_ref,
                 gain_ref, o_ref, stout_ref, st_scr):
    c = pl.program_id(1)

    @pl.when(c == 0)
    def _():
        st_scr[...] = st0_ref[...]

    st_new = [_hgrn_chunk(precise, n_levels, q_ref.at[u], lf_ref.at[u], k_ref.at[u], i_ref.at[u],
                          g_ref.at[u], cm_ref, mk_ref, e_ref, gain_ref, o_ref.at[u], st_scr.at[u])
              for u in range(q_ref.shape[0])]

    @pl.when(c == pl.num_programs(1) - 1)
    def _():
        for u in range(q_ref.shape[0]):
            stout_ref[u] = st_new[u].T


def _hgrn_chunk(precise, n_levels, q_ref, lf_ref, k_ref, i_ref, g_ref, cm_ref, mk_ref, e_ref,
                gain_ref, o_ref, st_scr):
    mx = _Mx(precise)

    cmat = cm_ref[0:(2 + n_levels) * CHUNK, :]
    if precise:
        p1, p2, p3 = _split3(lf_ref[...])
        r = _mm(cmat, jnp.concatenate([p1, p2, p3], axis=1))
        r = r[:, 0:256] + r[:, 256:512] + r[:, 512:768]
    else:
        p1, p2 = _split2(lf_ref[...])
        r = _mm(cmat, jnp.concatenate([p1, p2], axis=1))
        r = r[:, 0:256] + r[:, 256:512]
    q = q_ref[...]
    k = k_ref[...]
    iv = i_ref[...]
    b = r[0:CHUNK]
    su = r[CHUNK:2 * CHUNK]
    qd = q * jnp.exp(b)
    kd = k * jnp.exp(su)

    left = lax.broadcasted_iota(jnp.int32, (CHUNK, LANES), 1) < HEAD_DIM
    amat = [None] * 2
    for lv in range(n_levels):
        ex = jnp.exp(r[(lv + 2) * CHUNK:(lv + 3) * CHUNK])
        ql = q * ex
        kl = k * ex
        m2 = mk_ref[lv]
        for pr in range(2):
            qp = ql[:, pr * LANES:(pr + 1) * LANES]
            kp = kl[:, pr * LANES:(pr + 1) * LANES]
            q2 = jnp.concatenate([jnp.where(left, qp, 0.0), jnp.where(left, 0.0, qp)], axis=0)
            s = m2 * mx.mm_nt(q2, kp)
            amat[pr] = s if amat[pr] is None else amat[pr] + s

    e256 = e_ref[...]
    o_pairs = []
    for pr in range(2):
        o2 = mx.mm(amat[pr], iv[:, pr * LANES:(pr + 1) * LANES])
        o_pairs.append(jnp.where(left, o2[0:CHUNK], o2[CHUNK:2 * CHUNK]))
    o_intra = jnp.concatenate(o_pairs, axis=1)
    o_diag = _dot01_right(q * k, e256) * iv
    st = st_scr[...]
    o = mx.mm_nt(qd, st) + o_intra + o_diag
    o_ref[...] = (_group_rmsnorm(o, e256, gain_ref[...]) * g_ref[...]).astype(o_ref.dtype)

    upd = mx.mm(iv.T, kd)
    st_new = st * jnp.exp(b[CHUNK - 1:CHUNK, :]) + e256.astype(F32) * upd
    st_scr[...] = st_new
    return st_new


def _hgrn(precise, n_levels, qa, lfa, ka, ia, ga, st0, cm, mk, e256, gain, nb, nc):
    t = qa.shape[0]
    n = max(d for d in range(1, HGRN_SEQS_PER_STEP + 1) if nb % d == 0)
    row = pl.BlockSpec((n, None, CHUNK, A_WIDTH), lambda b, c: (b, c, 0, 0))
    full = lambda a: pl.BlockSpec(a.shape, lambda b, c: (0,) * a.ndim)
    st_spec = pl.BlockSpec((n, A_WIDTH, A_WIDTH), lambda b, c: (b, 0, 0))
    as4 = lambda a: a.reshape(nb, nc, CHUNK, A_WIDTH)
    o, st = pl.pallas_call(
        functools.partial(_hgrn_kernel, precise, n_levels),
        grid=(nb // n, nc),
        in_specs=[row] * 5 + [st_spec, full(cm), full(mk), full(e256), full(gain)],
        out_specs=[row, st_spec],
        out_shape=[jax.ShapeDtypeStruct((nb, nc, CHUNK, A_WIDTH), F32 if precise else BF16),
                   jax.ShapeDtypeStruct((nb, A_WIDTH, A_WIDTH), F32)],
        scratch_shapes=[pltpu.VMEM((n, A_WIDTH, A_WIDTH), F32)],
        compiler_params=pltpu.CompilerParams(
            dimension_semantics=("arbitrary", "arbitrary"), vmem_limit_bytes=VMEM_LIMIT),
        name="hgrn",
    )(as4(qa), as4(lfa), as4(ka), as4(ia), as4(ga), st0, cm, mk, e256, gain)
    return o.reshape(t, A_WIDTH), st


def _fox_kernel(q_ref, k_ref, v_ref, gain_ref, wms_ref, sh_ref, o_ref, m_scr, acc_scr):
    i = pl.program_id(1)
    tq, tk = ATT_TQ, ATT_TK
    m_scr[...] = jnp.full(m_scr.shape, NEG_INF, F32)
    acc_scr[...] = jnp.zeros(acc_scr.shape, F32)
    n_full = (i * tq) // tk

    def block(r0, nr, ks, mask):
        rows = slice(r0, r0 + nr)
        for hd in range(B_HEADS):
            lanes = slice(hd * LANES, (hd + 1) * LANES)
            s = _mm_nt(q_ref[rows, lanes], k_ref[pl.ds(ks, tk), lanes])
            if mask is not None:
                s = jnp.where(mask, s, NEG_INF)
            m_prev = m_scr[hd, rows, :]
            m_new = jnp.maximum(m_prev, jnp.max(s, axis=-1, keepdims=True))
            alpha = jnp.exp2(m_prev - m_new)
            p = jnp.exp2(s - jnp.concatenate([m_new] * (tk // LANES), axis=1))
            acc_scr[hd, rows, :] = (alpha * acc_scr[hd, rows, :]
                                    + _mm(p.astype(BF16), v_ref[pl.ds(ks, tk), lanes]))
            m_scr[hd, rows, :] = m_new

    def body(j, carry):
        block(0, tq, pl.multiple_of(j * tk, tk), None)
        return carry

    lax.fori_loop(0, n_full, body, 0)
    tri = (lax.broadcasted_iota(jnp.int32, (tk, tk), 1)
           <= lax.broadcasted_iota(jnp.int32, (tk, tk), 0))
    for d in range(tq // tk):
        ks = pl.multiple_of((n_full + d) * tk, tk)
        block(d * tk, tk, ks, tri)
        if (d + 1) * tk < tq:
            block((d + 1) * tk, tq - (d + 1) * tk, ks, None)

    left = lax.broadcasted_iota(jnp.int32, (tq, 2 * LANES), 1) % LANES < HEAD_DIM
    for pr in range(B_HEADS // 2):
        a2 = jnp.concatenate([acc_scr[2 * pr], acc_scr[2 * pr + 1]], axis=1)
        y = a2 * a2
        y = jnp.where(left, y, y * (HEAD_DIM * RMS_EPS))
        yh, yl = _split2(y)
        z = _mm(yh, wms_ref[...]) + _mm(yl, wms_ref[...])
        on = a2 * lax.rsqrt(z * (1.0 / HEAD_DIM)) * gain_ref[:, 2 * pr * LANES:(2 * pr + 2) * LANES]
        onb = on.astype(BF16)
        pair = onb[:, 0:LANES].astype(F32) + _mm(onb[:, LANES:2 * LANES], sh_ref[...])
        o_ref[:, pr * LANES:(pr + 1) * LANES] = pair.astype(BF16)


def _fox(qaug, kaug, vaug, gain, wms, sh, nb, seq):
    t = qaug.shape[0]
    nq = seq // ATT_TQ
    resident = lambda: pl.BlockSpec((seq, AUG_WIDTH), lambda b, i: (b, 0), pipeline_mode=pl.Buffered(1))
    return pl.pallas_call(
        _fox_kernel,
        grid=(nb, nq),
        in_specs=[
            pl.BlockSpec((ATT_TQ, AUG_WIDTH), lambda b, i: (b * nq + i, 0)),
            resident(), resident(),
            pl.BlockSpec(gain.shape, lambda b, i: (0, 0)),
            pl.BlockSpec(wms.shape, lambda b, i: (0, 0)),
            pl.BlockSpec(sh.shape, lambda b, i: (0, 0)),
        ],
        out_specs=pl.BlockSpec((ATT_TQ, B_WIDTH), lambda b, i: (b * nq + i, 0)),
        out_shape=jax.ShapeDtypeStruct((t, B_WIDTH), BF16),
        scratch_shapes=[
            pltpu.VMEM((B_HEADS, ATT_TQ, LANES), F32),
            pltpu.VMEM((B_HEADS, ATT_TQ, LANES), F32),
        ],
        compiler_params=pltpu.CompilerParams(
            dimension_semantics=("arbitrary", "arbitrary"), vmem_limit_bytes=VMEM_LIMIT),
        name="fox_prompt",
    )(qaug, kaug, vaug, gain, wms, sh)


def _paged_kernel(layer, g_pages, pt_ref, q_ref, ck_hbm, cv_hbm, clf_hbm, kn_ref, vn_ref, lfn_ref,
                  triu_ref, bm_ref, nm_ref, gain_ref, o_ref,
                  m_scr, l_scr, acc_scr, carry_scr, kbuf, vbuf, lfbuf, sem):
    b = pl.program_id(0)
    p = pl.program_id(1)
    nb = pl.num_programs(0)
    last = pl.num_programs(1) - 1
    rows = q_ref.shape[0]
    flat = B_HEADS * HEAD_DIM
    slot = (b * last + p) % 2

    def page_copies(bb, pp, sl):
        cps = []
        for g in range(g_pages):
            page = pt_ref[bb, pp * g_pages + g]
            cps.append(pltpu.make_async_copy(ck_hbm.at[layer, page], kbuf.at[sl, g], sem.at[sl]))
            cps.append(pltpu.make_async_copy(cv_hbm.at[layer, page], vbuf.at[sl, g], sem.at[sl]))
            cps.append(pltpu.make_async_copy(clf_hbm.at[layer, page], lfbuf.at[sl, g], sem.at[sl]))
        return cps

    @pl.when((b == 0) & (p == 0))
    def _():
        for cp in page_copies(0, 0, 0):
            cp.start()

    @pl.when(p + 1 < last)
    def _():
        for cp in page_copies(b, p + 1, 1 - slot):
            cp.start()

    @pl.when((p + 1 == last) & (b + 1 < nb))
    def _():
        for cp in page_copies(b + 1, 0, 1 - slot):
            cp.start()

    @pl.when(p == 0)
    def _():
        m_scr[...] = jnp.full(m_scr.shape, NEG_INF, F32)
        l_scr[...] = jnp.zeros(l_scr.shape, F32)
        acc_scr[...] = jnp.zeros(acc_scr.shape, F32)
        carry_scr[...] = jnp.zeros(carry_scr.shape, F32)

    qh, ql = _split2(q_ref[...])
    q2 = jnp.concatenate([qh, ql], axis=0)

    def step(pages, mask):
        carry = carry_scr[...]
        ss = []
        for k_ref, _, lf_ref in pages:
            kh, kl = _split2(k_ref[...].reshape(flat, CHUNK))
            lf = jnp.concatenate([lf_ref[...], jnp.zeros((SUBLANES, CHUNK), F32)], axis=0)
            c = carry + _dot01_right(lf, triu_ref[...])[0:B_HEADS]
            carry = jnp.broadcast_to(c[:, CHUNK - 1:CHUNK], carry.shape)
            sk = _mm(q2, kh)
            s = sk[0:rows] + sk[rows:2 * rows] + _mm(qh, kl)
            s = (s.reshape(B_HEADS, SUBLANES, CHUNK) - c[:, None, :]).reshape(rows, CHUNK)
            if mask is not None:
                s = jnp.where(mask > 0.0, s, NEG_INF)
            ss.append(s)
        carry_scr[...] = carry
        s_all = ss[0] if len(ss) == 1 else jnp.concatenate(ss, axis=1)
        m_prev = m_scr[...]
        m_new = jnp.maximum(m_prev, jnp.max(s_all, axis=-1, keepdims=True))
        alpha = jnp.exp(m_prev - m_new)
        p_all = jnp.exp(s_all - jnp.concatenate([m_new] * len(ss), axis=1))
        l_scr[...] = alpha * l_scr[...] + jnp.sum(p_all, axis=-1, keepdims=True)
        m_scr[...] = m_new
        pv = None
        for g, (_, v_ref, _) in enumerate(pages):
            vh, vl = _split2(v_ref[...].reshape(flat, CHUNK))
            ph, plo = _split2(p_all[:, g * CHUNK:(g + 1) * CHUNK])
            pk = _mm_nt(jnp.concatenate([ph, plo], axis=0), vh)
            d = pk[0:rows] + pk[rows:2 * rows] + _mm_nt(ph, vl)
            pv = d if pv is None else pv + d
        acc_scr[...] = jnp.concatenate([alpha] * (flat // LANES), axis=1) * acc_scr[...] + pv

    @pl.when(p < last)
    def _():
        for cp in page_copies(b, p, slot):
            cp.wait()
        step([(kbuf.at[slot, g], vbuf.at[slot, g], lfbuf.at[slot, g]) for g in range(g_pages)], None)

    @pl.when(p == last)
    def _():
        step([(kn_ref, vn_ref, lfn_ref)], nm_ref[...])
        o = acc_scr[...] / jnp.concatenate([l_scr[...]] * (flat // LANES), axis=1)
        o = o * bm_ref[...]
        ms = jnp.sum(o * o, axis=-1, keepdims=True) * (1.0 / HEAD_DIM)
        on = o * lax.rsqrt(ms + RMS_EPS) * gain_ref[...]
        o_ref[...] = jnp.sum(on.reshape(B_HEADS, SUBLANES, flat), axis=0)


def _paged(layer, page_table, q_bd, ck, cv, clf, kn, vn, lfn, triu, bm, nm, gain):
    nb, n_pages = page_table.shape
    rows = q_bd.shape[1]
    g_pages = min(PAGES_PER_STEP, n_pages)
    assert n_pages % g_pages == 0
    n_steps = n_pages // g_pages

    kv_minor = (B_HEADS, HEAD_DIM, CHUNK)
    full2 = lambda a: pl.BlockSpec(a.shape, lambda b, p, pt: (0, 0))
    in_specs = [pl.BlockSpec((None, rows, B_WIDTH), lambda b, p, pt: (b, 0, 0))]
    in_specs += [pl.BlockSpec(memory_space=pl.ANY)] * 3
    in_specs += [
        pl.BlockSpec((None,) + kv_minor, lambda b, p, pt: (b, 0, 0, 0)),
        pl.BlockSpec((None,) + kv_minor, lambda b, p, pt: (b, 0, 0, 0)),
        pl.BlockSpec((None, B_HEADS, CHUNK), lambda b, p, pt: (b, 0, 0)),
        full2(triu), full2(bm), full2(nm), full2(gain),
    ]
    grid_spec = pltpu.PrefetchScalarGridSpec(
        num_scalar_prefetch=1,
        grid=(nb, n_steps + 1),
        in_specs=in_specs,
        out_specs=pl.BlockSpec((None, SUBLANES, B_WIDTH), lambda b, p, pt: (b, 0, 0)),
        scratch_shapes=[
            pltpu.VMEM((rows, LANES), F32),
            pltpu.VMEM((rows, LANES), F32),
            pltpu.VMEM((rows, B_WIDTH), F32),
            pltpu.VMEM((B_HEADS, CHUNK), F32),
            pltpu.VMEM((2, g_pages) + kv_minor, F32),
            pltpu.VMEM((2, g_pages) + kv_minor, F32),
            pltpu.VMEM((2, g_pages, B_HEADS, CHUNK), F32),
            pltpu.SemaphoreType.DMA((2,)),
        ],
    )
    return pl.pallas_call(
        functools.partial(_paged_kernel, layer, g_pages),
        grid_spec=grid_spec,
        out_shape=jax.ShapeDtypeStruct((nb, SUBLANES, B_WIDTH), F32),
        compiler_params=pltpu.CompilerParams(
            dimension_semantics=("arbitrary", "arbitrary"), vmem_limit_bytes=VMEM_LIMIT),
        name="fox_paged",
    )(page_table, q_bd, ck, cv, clf, kn, vn, lfn, triu, bm, nm, gain)


def _outproj_kernel(precise, x_ref, oa_ref, ob_ref, oc_ref, w_ref, g_ref, wrt_ref, brc_ref,
                    x2_ref, h2_ref, comb_ref):
    mx = _Mx(precise)
    tm = x_ref.shape[0]
    mix = (mx.mm(oa_ref[...], w_ref[0:A_WIDTH, :])
           + mx.mm(ob_ref[...], w_ref[A_WIDTH:A_WIDTH + B_WIDTH, :])
           + mx.mm(oc_ref[...], w_ref[A_WIDTH + B_WIDTH:, :]))
    x2 = x_ref[...] + mix
    x2_ref[...] = x2
    h2 = _rmsnorm_rows(x2, g_ref[...])
    h2b = h2.astype(BF16)
    h2_ref[...] = h2b
    h2l = (h2 - h2b.astype(F32)).astype(BF16)
    wr = wrt_ref[...]
    wrh = wr.astype(BF16)
    wrl = (wr - wrh.astype(F32)).astype(BF16)
    lg = _mm_nt(wrh, h2b) + _mm_nt(wrh, h2l) + _mm_nt(wrl, h2b) + brc_ref[...]
    row = [lg[i:i + 1, :] for i in range(N_GROUPS + N_EXPERTS)]

    g = row[0:N_GROUPS]
    gmax = jnp.maximum(jnp.maximum(g[0], g[1]), jnp.maximum(g[2], g[3]))
    gsel = jnp.where(g[0] == gmax, 0, jnp.where(g[1] == gmax, 1, jnp.where(g[2] == gmax, 2, 3)))
    den = (jnp.exp(g[0] - gmax) + jnp.exp(g[1] - gmax)) + (jnp.exp(g[2] - gmax) + jnp.exp(g[3] - gmax))
    gate = 1.0 / den
    le = []
    for e in range(N_EXP):
        le.append(jnp.where(gsel == 0, row[4 + e],
                            jnp.where(gsel == 1, row[8 + e],
                                      jnp.where(gsel == 2, row[12 + e], row[16 + e]))))
    v1 = jnp.maximum(jnp.maximum(le[0], le[1]), jnp.maximum(le[2], le[3]))
    i1 = jnp.where(le[0] == v1, 0, jnp.where(le[1] == v1, 1, jnp.where(le[2] == v1, 2, 3)))
    le2 = [jnp.where(i1 == e, -jnp.inf, le[e]) for e in range(N_EXP)]
    v2 = jnp.maximum(jnp.maximum(le2[0], le2[1]), jnp.maximum(le2[2], le2[3]))
    i2 = jnp.where(le2[0] == v2, 0, jnp.where(le2[1] == v2, 1, jnp.where(le2[2] == v2, 2, 3)))
    ex = jnp.exp(v2 - v1)
    w1 = 1.0 / (1.0 + ex)
    w2 = ex * w1
    rid = lax.broadcasted_iota(jnp.int32, (N_EXPERTS, tm), 0)
    combt = jnp.zeros((N_EXPERTS, tm), F32)
    for gi in range(N_GROUPS):
        for e in range(N_EXP):
            fine = jnp.where(i1 == e, w1, 0.0) + jnp.where(i2 == e, w2, 0.0)
            val = jnp.where(gsel == gi, gate * fine, 0.0)
            combt = jnp.where(rid == gi * N_EXP + e, val, combt)
    combt = jnp.concatenate([combt, jnp.zeros((LANES - N_EXPERTS, tm), F32)], axis=0)
    comb_ref[...] = combt.T


def _outproj(precise, x2d, oa, ob, oc, w_out, g, wrt, brc, tm):
    t = x2d.shape[0]
    row = lambda w: pl.BlockSpec((tm, w), lambda i: (i, 0))
    full = lambda a: pl.BlockSpec(a.shape, lambda i: (0,) * a.ndim)
    return pl.pallas_call(
        functools.partial(_outproj_kernel, precise),
        grid=(t // tm,),
        in_specs=[row(D_MODEL), row(A_WIDTH), row(B_WIDTH), row(C_WIDTH),
                  full(w_out), full(g), full(wrt), full(brc)],
        out_specs=[row(D_MODEL), row(D_MODEL), row(LANES)],
        out_shape=[jax.ShapeDtypeStruct((t, D_MODEL), F32),
                   jax.ShapeDtypeStruct((t, D_MODEL), BF16),
                   jax.ShapeDtypeStruct((t, LANES), F32)],
        compiler_params=pltpu.CompilerParams(
            dimension_semantics=("arbitrary",), vmem_limit_bytes=VMEM_LIMIT),
        name="outproj_router",
    )(x2d, oa, ob, oc, w_out, g, wrt, brc)


MOE_EXPERTS_PER_STEP = 4


def _moe_kernel(final, x2_ref, h2_ref, comb_ref, wg_ref, wu_ref, wd_ref, gf_ref, y_ref, acc_scr):
    e = pl.program_id(1)

    @pl.when(e == 0)
    def _():
        acc_scr[...] = x2_ref[...]

    h = h2_ref[...]
    comb = comb_ref[...]
    lane = lax.broadcasted_iota(jnp.int32, comb.shape, 1)
    tot = None
    for j in range(wg_ref.shape[0]):
        a = _mm(h, wg_ref[j])
        u = _mm(h, wu_ref[j])
        cw = jnp.sum(jnp.where(lane == e * wg_ref.shape[0] + j, comb, 0.0), axis=-1, keepdims=True)
        act = (a * _sigmoid(a)) * u * cw
        d = _mm(act.astype(BF16), wd_ref[j])
        tot = d if tot is None else tot + d
    acc_scr[...] += tot

    @pl.when(e == pl.num_programs(1) - 1)
    def _():
        y = acc_scr[...]
        if final:
            y = _rmsnorm_rows(y, gf_ref[...])
        y_ref[...] = y


def _moe(final, x2, h2, comb, wg, wu, wd, gf, tm):
    t = x2.shape[0]
    return pl.pallas_call(
        functools.partial(_moe_kernel, final),
        grid=(t // tm, N_EXPERTS // MOE_EXPERTS_PER_STEP),
        in_specs=[
            pl.BlockSpec((tm, D_MODEL), lambda i, e: (i, 0)),
            pl.BlockSpec((tm, D_MODEL), lambda i, e: (i, 0)),
            pl.BlockSpec((tm, LANES), lambda i, e: (i, 0)),
            pl.BlockSpec((MOE_EXPERTS_PER_STEP, D_MODEL, EXPERT_FF), lambda i, e: (e, 0, 0)),
            pl.BlockSpec((MOE_EXPERTS_PER_STEP, D_MODEL, EXPERT_FF), lambda i, e: (e, 0, 0)),
            pl.BlockSpec((MOE_EXPERTS_PER_STEP, EXPERT_FF, D_MODEL), lambda i, e: (e, 0, 0)),
            pl.BlockSpec(gf.shape, lambda i, e: (0, 0)),
        ],
        out_specs=pl.BlockSpec((tm, D_MODEL), lambda i, e: (i, 0)),
        out_shape=jax.ShapeDtypeStruct((t, D_MODEL), F32),
        scratch_shapes=[pltpu.VMEM((tm, D_MODEL), F32)],
        compiler_params=pltpu.CompilerParams(
            dimension_semantics=("arbitrary", "arbitrary"), vmem_limit_bytes=VMEM_LIMIT),
        name="moe",
    )(x2, h2, comb, wg, wu, wd, gf)


def _block_diag_state(s):
    n = s.shape[0]
    out = jnp.zeros((n, A_HEADS, HEAD_DIM, A_HEADS, HEAD_DIM), s.dtype)
    st = jnp.swapaxes(s, -1, -2)
    for h in range(A_HEADS):
        out = out.at[:, h, :, h, :].set(st[:, h])
    return out.reshape(n, A_WIDTH, A_WIDTH)


def _state_blocks(sbd):
    n = sbd.shape[0]
    s5 = sbd.reshape(n, A_HEADS, HEAD_DIM, A_HEADS, HEAD_DIM)
    return jnp.stack([s5[:, h, :, h, :] for h in range(A_HEADS)], axis=1)


def _pad_rows(a, nb, per, to):
    w = a.shape[-1]
    a3 = a.reshape(nb, per, w)
    return jnp.pad(a3, ((0, 0), (0, to - per), (0, 0))).reshape(nb * to, w)


def kernel(x_prompt, x_sample, cache_k, cache_v, cache_logf, state_hgrn, page_table, norm_mix_gain, w_in, hgrn_lb_logits, hgrn_out_gain, fox_f_bias, fox_out_gain, sgu_v_gain, sgu_w_s, sgu_b, sgu_out_gain, w_out, norm_ffn_gain, router_group_w, router_group_b, router_expert_w, router_expert_b, expert_w_gate, expert_w_up, expert_w_down, norm_final_gain):
    nb, seq, _ = x_prompt.shape
    ndb, dseq, _ = x_sample.shape
    depth = w_in.shape[0]
    tp = nb * seq
    ts = ndb * dseq
    tm_p = 512
    assert seq % ATT_TQ == 0 and ATT_TQ % ATT_TK == 0 and seq % tm_p == 0
    assert ts % SUBLANES == 0 and ts <= CHUNK and dseq <= SUBLANES

    cm_np, mk_np = _hgrn_consts()
    cm = jnp.asarray(cm_np, BF16)
    mk = jnp.asarray(mk_np, F32)
    e256 = jnp.asarray(_block_ones(A_WIDTH, HEAD_DIM), BF16)
    place_np, qone_np, vone_np = _aug_consts()
    place = jnp.asarray(place_np, BF16)
    wms_np = np.zeros((2 * LANES, 2 * LANES), np.float32)
    sh_np = np.zeros((LANES, LANES), np.float32)
    for blk in range(2):
        wms_np[blk * LANES:blk * LANES + HEAD_DIM + 1, blk * LANES:(blk + 1) * LANES] = 1.0
    sh_np[np.arange(HEAD_DIM), HEAD_DIM + np.arange(HEAD_DIM)] = 1.0
    wms = jnp.asarray(wms_np, BF16)
    sh = jnp.asarray(sh_np, BF16)
    qone = jnp.asarray(qone_np)
    vone = jnp.asarray(vone_np)
    ar = np.arange(tm_p)
    tril_tm = jnp.asarray((ar[None, :] <= ar[:, None]).astype(np.float32), BF16)
    ac = np.arange(CHUNK)
    triu_pg = jnp.asarray((ac[:, None] <= ac[None, :]).astype(np.float32), BF16)
    rows = B_HEADS * SUBLANES
    rr = np.arange(rows)
    col = np.arange(B_WIDTH)
    bm = jnp.asarray((rr[:, None] // SUBLANES == col[None, :] // HEAD_DIM).astype(np.float32))
    tq_of_row = rr[:, None] % SUBLANES
    nm = jnp.asarray(((ac[None, :] < dseq) & (ac[None, :] <= tq_of_row)
                      & (tq_of_row < dseq)).astype(np.float32))

    sr = np.arange(ts)
    same_seq = jnp.asarray(((sr[:, None] // dseq) == (sr[None, :] // dseq))
                           & ((sr[None, :] % dseq) <= (sr[:, None] % dseq)))
    tril = jnp.asarray(np.tril(np.ones((CHUNK, CHUNK), np.float32)))

    ck = jnp.transpose(cache_k, (0, 1, 3, 4, 2))
    cv = jnp.transpose(cache_v, (0, 1, 3, 4, 2))
    clf = jnp.transpose(cache_logf, (0, 1, 3, 2))

    xp = x_prompt.reshape(tp, D_MODEL)
    xs = x_sample.reshape(ts, D_MODEL)
    zeros_state = jnp.zeros((nb, A_WIDTH, A_WIDTH), F32)
    n_levels_s = sum(1 for h in HGRN_LEVELS if h < dseq)

    outs_p = {k: [] for k in ("lf", "s")}
    kv_p = None
    outs_s = {k: [] for k in ("k", "v", "lf", "s", "vn")}

    for l in range(depth):
        wl = w_in[l]
        w_r32 = jnp.concatenate(
            [wl[:, 0:2560], wl[:, 2568:3080], wl[:, 2560:2568],
             jnp.zeros((D_MODEL, IN_COLS_PAD - 3080), F32)], axis=1)
        w_r = w_r32.astype(BF16)
        fb = jnp.pad(fox_f_bias[l][None, :], ((0, 0), (0, LANES - B_HEADS)))
        g_mix = norm_mix_gain[l][None, :]
        vg = sgu_v_gain[l][None, :]
        sog = sgu_out_gain[l][None, :]
        smat_p = (sgu_w_s[l] * tril).astype(BF16)
        sbias_p = jnp.repeat(sgu_b[l].T, HEAD_DIM, axis=1)
        w4 = sgu_w_s[l][:, :dseq, :dseq]
        smat_s = jnp.where(same_seq, jnp.tile(w4, (1, ndb, ndb)), 0.0)
        sbias_s = jnp.repeat(jnp.tile(sgu_b[l][:, :dseq].T, (ndb, 1)), HEAD_DIM, axis=1)
        hg = hgrn_out_gain[l][None, :]
        fg = fox_out_gain[l][None, :]
        fg_blk = jnp.pad(fox_out_gain[l].reshape(B_HEADS, HEAD_DIM),
                         ((0, 0), (0, LANES - HEAD_DIM))).reshape(1, AUG_WIDTH)
        wo32 = w_out[l]
        wo = wo32.astype(BF16)
        g_ffn = norm_ffn_gain[l][None, :]
        wrt = jnp.concatenate(
            [router_group_w[l].T,
             jnp.transpose(router_expert_w[l], (0, 2, 1)).reshape(N_EXPERTS, D_MODEL),
             jnp.zeros((32 - N_GROUPS - N_EXPERTS, D_MODEL), F32)], axis=0)
        brc = jnp.concatenate(
            [router_group_b[l], router_expert_b[l].reshape(-1),
             jnp.zeros((32 - N_GROUPS - N_EXPERTS,), F32)])[:, None]
        wg = expert_w_gate[l].reshape(N_EXPERTS, D_MODEL, EXPERT_FF).astype(BF16)
        wu = expert_w_up[l].reshape(N_EXPERTS, D_MODEL, EXPERT_FF).astype(BF16)
        wd = expert_w_down[l].reshape(N_EXPERTS, EXPERT_FF, D_MODEL).astype(BF16)
        gf = norm_final_gain[None, :]
        final = l == depth - 1

        consts_p = (g_mix, w_r, hgrn_lb_logits, fb, vg, smat_p, sbias_p, sog, e256,
                    tril_tm, place, qone, vone)
        (qa, lfa, ka, ia, ga, _, kb, vb, lfb, oc, _, qaug, kaug, vaug) = _inproj(
            l, True, seq // tm_p, False, xp, consts_p, tm_p, depth, kv_p)
        kv_p = (kb, vb)
        oa, st = _hgrn(False, len(HGRN_LEVELS), qa, lfa, ka, ia, ga, zeros_state, cm, mk, e256, hg, nb, seq // CHUNK)
        ob = _fox(qaug, kaug, vaug, fg_blk, wms, sh, nb, seq)
        x2, h2, comb = _outproj(False, xp, oa, ob, oc, wo, g_ffn, wrt, brc, tm_p)
        xp = _moe(final, x2, h2, comb, wg, wu, wd, gf, 1024 if tp % 1024 == 0 else tm_p)
        outs_p["lf"].append(lfb.reshape(nb, seq, B_HEADS))
        outs_p["s"].append(_state_blocks(st))

        consts_s = (g_mix, w_r32, hgrn_lb_logits, fb, vg, smat_s, sbias_s, sog, e256,
                    tril_tm, place, qone, vone)
        (qa, lfa, ka, ia, ga, qb, kb, vb, lfb, oc, vn, _, _, _) = _inproj(
            l, False, 1, True, xs, consts_s, ts, 1, None)
        pads = [_pad_rows(a, ndb, dseq, CHUNK) for a in (qa, lfa, ka, ia, ga)]
        oa_pad, st = _hgrn(True, n_levels_s, *pads, _block_diag_state(state_hgrn[l]), cm, mk, e256, hg, ndb, 1)
        oa = oa_pad.reshape(ndb, CHUNK, A_WIDTH)[:, :dseq].reshape(ts, A_WIDTH)
        q4 = jnp.pad(qb.reshape(ndb, dseq, B_HEADS, HEAD_DIM),
                     ((0, 0), (0, SUBLANES - dseq), (0, 0), (0, 0)))
        q_bd = (jnp.transpose(q4, (0, 2, 1, 3))[:, :, :, None, :]
                * jnp.eye(B_HEADS, dtype=F32)[None, :, None, :, None]).reshape(ndb, rows, B_WIDTH)
        k5 = kb.reshape(ndb, dseq, B_HEADS, HEAD_DIM)
        v5 = vb.reshape(ndb, dseq, B_HEADS, HEAD_DIM)
        lf3 = lfb.reshape(ndb, dseq, B_HEADS)
        padn = ((0, 0), (0, 0), (0, 0), (0, CHUNK - dseq))
        ob8 = _paged(l, page_table, q_bd, ck, cv, clf,
                     jnp.pad(jnp.transpose(k5, (0, 2, 3, 1)), padn),
                     jnp.pad(jnp.transpose(v5, (0, 2, 3, 1)), padn),
                     jnp.pad(jnp.transpose(lf3, (0, 2, 1)), padn[1:]),
                     triu_pg, bm, nm, fg)
        ob = ob8[:, :dseq].reshape(ts, B_WIDTH)
        x2, h2, comb = _outproj(True, xs, oa, ob, oc, wo32, g_ffn, wrt, brc, ts)
        xs = _moe(final, x2, h2, comb, wg, wu, wd, gf, ts)
        outs_s["k"].append(k5)
        outs_s["v"].append(v5)
        outs_s["lf"].append(lf3)
        outs_s["s"].append(_state_blocks(st))
        outs_s["vn"].append(vn.reshape(ndb, dseq, C_WIDTH))

    return (xp.reshape(nb, seq, D_MODEL), xs.reshape(ndb, dseq, D_MODEL),
            jnp.transpose(kv_p[0].reshape(depth, nb, B_HEADS, HEAD_DIM, seq), (0, 1, 4, 2, 3)),
            jnp.transpose(kv_p[1].reshape(depth, nb, B_HEADS, HEAD_DIM, seq), (0, 1, 4, 2, 3)),
            jnp.stack(outs_p["lf"]),
            jnp.stack(outs_p["s"]),
            jnp.stack(outs_s["k"]), jnp.stack(outs_s["v"]), jnp.stack(outs_s["lf"]),
            jnp.stack(outs_s["s"]), jnp.stack(outs_s["vn"]))
```
